```python
import jax, jax.numpy as jnp
from jax import lax
import numpy as np

D_MODEL = 1024
BATCH = 2
SEQ = 8192
DEPTH = 2
DEC_BATCH = 32
DEC_SEQ = 32
PAST_LEN = 4096

CHUNK = 64
Q_BLOCK = 128
N_HEADS = 8
N_KV = 2
HEAD_DIM = 64
ROT_DIM = HEAD_DIM // 4
ROPE_THETA = 500000.0
IDX_HEADS = 4
IDX_DIM = 64
IDX_ROT = IDX_DIM // 4
TOPK_MAX = 256
M_HEADS = 4
M_DIM = 128
D_FF = 2816
EPS = 1e-6
M_NEG = -1e30
SPLIT_SIZES = (N_HEADS * HEAD_DIM, N_KV * HEAD_DIM, N_KV * HEAD_DIM,
               IDX_HEADS * IDX_DIM, IDX_DIM, IDX_HEADS,
               M_HEADS * M_DIM, M_HEADS * M_DIM, M_HEADS * M_DIM, M_HEADS, M_HEADS, M_HEADS * M_DIM,
               D_MODEL, D_MODEL)
IN_COLS = sum(SPLIT_SIZES)

kernel_name = 'dsa_mlstm_macaron_stream_step'

F32 = jnp.float32


def rmsnorm(x, g):
    x32 = x.astype(F32)
    y = x32 * lax.rsqrt(jnp.mean(x32 * x32, axis=-1, keepdims=True) + EPS)
    return (y * g.astype(F32)).astype(x.dtype)


def swiglu(h, w_gate, w_up, w_down):
    return (jax.nn.silu(h @ w_gate) * (h @ w_up)) @ w_down


def rope(x, pos, rot):
    half = rot // 2
    freqs = ROPE_THETA ** (-jnp.arange(half, dtype=F32) / half)
    ang = pos.astype(F32)[:, None] * freqs[None, :]
    cos = jnp.cos(ang)[:, None, :]
    sin = jnp.sin(ang)[:, None, :]
    x1 = x[..., :half].astype(F32)
    x2 = x[..., half:rot].astype(F32)
    r1 = (x1 * cos - x2 * sin).astype(x.dtype)
    r2 = (x2 * cos + x1 * sin).astype(x.dtype)
    return jnp.concatenate([r1, r2, x[..., rot:]], axis=-1)


def project(h, w_in, pos):
    B, T = h.shape[:2]
    z = h @ w_in
    cuts = [int(c) for c in np.cumsum(SPLIT_SIZES)[:-1]]
    q, k, v, qi, ki, wi, mq, mk, mv, mi, mf, mo, ga, gm = jnp.split(z, cuts, axis=-1)
    q = rope(q.reshape(B, T, N_HEADS, HEAD_DIM), pos, ROT_DIM)
    k = rope(k.reshape(B, T, N_KV, HEAD_DIM), pos, ROT_DIM)
    v = v.reshape(B, T, N_KV, HEAD_DIM)
    qi = rope(qi.reshape(B, T, IDX_HEADS, IDX_DIM), pos, IDX_ROT)
    ki = rope(ki.reshape(B, T, 1, IDX_DIM), pos, IDX_ROT).reshape(B, T, IDX_DIM)
    return q, k, v, qi, ki, wi, mq, mk, mv, mi, mf, mo, ga, gm


def dsa_block(q, qi, wi, qpos, k, v, ki, kpos, n_sel):
    B, Tq = q.shape[:2]
    allowed = (kpos[None, :] // CHUNK) <= (qpos[:, None] // CHUNK)
    logits = jnp.einsum('bthd,bsd->bths', qi.astype(F32), ki.astype(F32)) * (IDX_DIM ** -0.5)
    score = jnp.einsum('bths,bth->bts', jax.nn.relu(logits), wi.astype(F32)) * (IDX_HEADS ** -0.5)
    score = jnp.where(allowed[None], score, -jnp.inf)
    vals, idx = lax.top_k(score, n_sel)
    valid = jnp.isfinite(vals)
    gather = jax.vmap(lambda a, i: a[i])
    k_sel = gather(k, idx)
    v_sel = gather(v, idx)
    qg = q.reshape(B, Tq, N_KV, N_HEADS // N_KV, HEAD_DIM)
    s = jnp.einsum('btngd,btjnd->btngj', qg, k_sel, preferred_element_type=F32) * (HEAD_DIM ** -0.5)
    s = jnp.where(valid[:, :, None, None, :], s, -jnp.inf)
    p = jax.nn.softmax(s, axis=-1).astype(v.dtype)
    o = jnp.einsum('btngj,btjnd->btngd', p, v_sel)
    return o.reshape(B, Tq, N_HEADS * HEAD_DIM)


def dsa_prompt(q, qi, wi, k, v, ki, pos):
    B, S = q.shape[:2]
    nb = S // Q_BLOCK
    n_sel = min(TOPK_MAX, S // 4)

    def blk(a):
        return jnp.moveaxis(a.reshape((B, nb, Q_BLOCK) + a.shape[2:]), 1, 0)

    out = lax.map(lambda args: dsa_block(*args, k, v, ki, pos, n_sel),
                  (blk(q), blk(qi), blk(wi), pos.reshape(nb, Q_BLOCK)))
    return jnp.moveaxis(out, 0, 1).reshape(B, S, N_HEADS * HEAD_DIM)


def mlstm_inputs(mq, mk, mv, mi, mf, b_i, b_f):
    B, T = mq.shape[:2]

    def heads(a):
        return a.reshape(B, T, M_HEADS, M_DIM).transpose(0, 2, 1, 3).astype(F32)

    q = heads(mq)
    k = heads(mk) * (M_DIM ** -0.5)
    v = heads(mv)
    ig = (mi.astype(F32) + b_i.astype(F32)).transpose(0, 2, 1)
    lf = jax.nn.log_sigmoid(mf.astype(F32) + b_f.astype(F32)).transpose(0, 2, 1)
    return q, k, v, ig, lf


def mlstm_chunk(carry, inp):
    C, n, m = carry
    q, k, v, ig, lf = inp
    T = q.shape[2]
    b = jnp.cumsum(lf, axis=-1)
    causal = jnp.tril(jnp.ones((T, T), dtype=bool))
    d = jnp.where(causal, b[..., :, None] - b[..., None, :] + ig[..., None, :], -jnp.inf)
    m_inter = b + m[..., None]
    m_t = jnp.maximum(m_inter, jnp.max(d, axis=-1))
    sc = jnp.einsum('bhtd,bhsd->bhts', q, k) * jnp.exp(d - m_t[..., None])
    decay = jnp.exp(m_inter - m_t)
    num = jnp.einsum('bhts,bhsd->bhtd', sc, v) + decay[..., None] * jnp.einsum('bhtk,bhvk->bhtv', q, C)
    den = jnp.sum(sc, axis=-1) + decay * jnp.einsum('bhtk,bhk->bht', q, n)
    h = num / jnp.maximum(jnp.abs(den), jnp.exp(-m_t))[..., None]
    m_new = m_t[..., -1]
    g = jnp.exp(b[..., -1:] - b + ig - m_new[..., None])
    dec_end = jnp.exp(b[..., -1] + m - m_new)
    C_new = dec_end[..., None, None] * C + jnp.einsum('bhs,bhsv,bhsk->bhvk', g, v, k)
    n_new = dec_end[..., None] * n + jnp.einsum('bhs,bhsk->bhk', g, k)
    return (C_new, n_new, m_new), h


def mlstm_prompt(q, k, v, ig, lf):
    B, H, S, _ = q.shape
    nc = S // CHUNK

    def ch(a):
        return jnp.moveaxis(a.reshape(a.shape[:2] + (nc, CHUNK) + a.shape[3:]), 2, 0)

    init = (jnp.zeros((B, H, M_DIM, M_DIM), F32), jnp.zeros((B, H, M_DIM), F32),
            jnp.full((B, H), M_NEG, F32))
    carry, hs = lax.scan(mlstm_chunk, init, (ch(q), ch(k), ch(v), ch(ig), ch(lf)))
    h = jnp.moveaxis(hs, 0, 2).reshape(B, H, S, M_DIM)
    return h, carry


def mlstm_readout(h, mo, g):
    B, H, T, d = h.shape
    hn = h * lax.rsqrt(jnp.mean(h * h, axis=-1, keepdims=True) + EPS) * g.astype(F32).reshape(H, 1, d)
    hn = hn.transpose(0, 2, 1, 3).reshape(B, T, H * d)
    return (jax.nn.sigmoid(mo.astype(F32)) * hn).astype(mo.dtype)


def merge(attn, mh, mo, ga, gm, g_ml, w_ap, w_mp, w_o):
    a = attn @ w_ap
    b = mlstm_readout(mh, mo, g_ml) @ w_mp
    return (jax.nn.sigmoid(ga) * a + jax.nn.sigmoid(gm) * b) @ w_o


def setup_inputs(seed: int = 0) -> dict:
    key = jax.random.key(seed)
    ks = jax.random.split(key, 26)
    nrm = lambda k, shape, scale: scale * jax.random.normal(k, shape, F32)
    gain = lambda k, shape: 1.0 + 0.05 * jax.random.normal(k, shape, F32)
    HA = N_HEADS * HEAD_DIM
    HM = M_HEADS * M_DIM
    return {
        'x_prompt': nrm(ks[0], (BATCH, SEQ, D_MODEL), 1.0),
        'x_sample': nrm(ks[1], (DEC_BATCH, DEC_SEQ, D_MODEL), 1.0),
        'cache_k': nrm(ks[2], (DEPTH, DEC_BATCH, PAST_LEN, N_KV, HEAD_DIM), 1.0),
        'cache_v': nrm(ks[3], (DEPTH, DEC_BATCH, PAST_LEN, N_KV, HEAD_DIM), 1.0),
        'cache_idx_k': nrm(ks[4], (DEPTH, DEC_BATCH, PAST_LEN, IDX_DIM), 1.0),
        'state_C': nrm(ks[5], (DEPTH, DEC_BATCH, M_HEADS, M_DIM, M_DIM), 1.0),
        'state_n': nrm(ks[6], (DEPTH, DEC_BATCH, M_HEADS, M_DIM), 1.0),
        'state_m': nrm(ks[7], (DEPTH, DEC_BATCH, M_HEADS), 1.0),
        'g_ffn1': gain(ks[8], (DEPTH, D_MODEL)),
        'w_ffn1_gate': nrm(ks[9], (DEPTH, D_MODEL, D_FF), D_MODEL ** -0.5),
        'w_ffn1_up': nrm(ks[10], (DEPTH, D_MODEL, D_FF), D_MODEL ** -0.5),
        'w_ffn1_down': nrm(ks[11], (DEPTH, D_FF, D_MODEL), D_FF ** -0.5),
        'g_mix': gain(ks[12], (DEPTH, D_MODEL)),
        'w_in': nrm(ks[13], (DEPTH, D_MODEL, IN_COLS), D_MODEL ** -0.5),
        'b_igate': nrm(ks[14], (DEPTH, M_HEADS), 0.1),
        'b_fgate': 3.0 + nrm(ks[15], (DEPTH, M_HEADS), 0.5),
        'g_mlstm': gain(ks[16], (DEPTH, HM)),
        'w_attn_proj': nrm(ks[17], (DEPTH, HA, D_MODEL), HA ** -0.5),
        'w_mlstm_proj': nrm(ks[18], (DEPTH, HM, D_MODEL), HM ** -0.5),
        'w_out': nrm(ks[19], (DEPTH, D_MODEL, D_MODEL), D_MODEL ** -0.5),
        'g_ffn2': gain(ks[20], (DEPTH, D_MODEL)),
        'w_ffn2_gate': nrm(ks[21], (DEPTH, D_MODEL, D_FF), D_MODEL ** -0.5),
        'w_ffn2_up': nrm(ks[22], (DEPTH, D_MODEL, D_FF), D_MODEL ** -0.5),
        'w_ffn2_down': nrm(ks[23], (DEPTH, D_FF, D_MODEL), D_FF ** -0.5),
        'g_final': gain(ks[24], (D_MODEL,)),
    }


def reference(x_prompt, x_sample, cache_k, cache_v, cache_idx_k, state_C, state_n, state_m,
              g_ffn1, w_ffn1_gate, w_ffn1_up, w_ffn1_down, g_mix, w_in, b_igate, b_fgate,
              g_mlstm, w_attn_proj, w_mlstm_proj, w_out, g_ffn2, w_ffn2_gate, w_ffn2_up,
              w_ffn2_down, g_final):
    S = x_prompt.shape[1]
    T = x_sample.shape[1]
    pos_p = jnp.arange(S, dtype=jnp.int32)
    pos_s = PAST_LEN + jnp.arange(T, dtype=jnp.int32)
    kpos_s = jnp.arange(PAST_LEN + T, dtype=jnp.int32)
    n_sel_s = min(TOPK_MAX, (PAST_LEN + T) // 4)
    xp, xs = x_prompt, x_sample
    kp, vp, ikp, Cp, nP, mP = [], [], [], [], [], []
    kS, vS, ikS, CS, nS, mS = [], [], [], [], [], []
    for l in range(DEPTH):
        ffn1 = (w_ffn1_gate[l], w_ffn1_up[l], w_ffn1_down[l])
        ffn2 = (w_ffn2_gate[l], w_ffn2_up[l], w_ffn2_down[l])
        mw = (g_mlstm[l], w_attn_proj[l], w_mlstm_proj[l], w_out[l])

        xp = xp + 0.5 * swiglu(rmsnorm(xp, g_ffn1[l]), *ffn1)
        q, k, v, qi, ki, wi, mq, mk, mv, mi, mf, mo, ga, gm = project(rmsnorm(xp, g_mix[l]), w_in[l], pos_p)
        attn = dsa_prompt(q, qi, wi, k, v, ki, pos_p)
        mh, (C, n, m) = mlstm_prompt(*mlstm_inputs(mq, mk, mv, mi, mf, b_igate[l], b_fgate[l]))
        xp = xp + merge(attn, mh, mo, ga, gm, *mw)
        xp = xp + 0.5 * swiglu(rmsnorm(xp, g_ffn2[l]), *ffn2)
        kp.append(k); vp.append(v); ikp.append(ki); Cp.append(C); nP.append(n); mP.append(m)

        xs = xs + 0.5 * swiglu(rmsnorm(xs, g_ffn1[l]), *ffn1)
        q, k, v, qi, ki, wi, mq, mk, mv, mi, mf, mo, ga, gm = project(rmsnorm(xs, g_mix[l]), w_in[l], pos_s)
        k_all = jnp.concatenate([cache_k[l].astype(k.dtype), k], axis=1)
        v_all = jnp.concatenate([cache_v[l].astype(v.dtype), v], axis=1)
        ik_all = jnp.concatenate([cache_idx_k[l].astype(ki.dtype), ki], axis=1)
        attn = dsa_block(q, qi, wi, pos_s, k_all, v_all, ik_all, kpos_s, n_sel_s)
        carry0 = (state_C[l].astype(F32), state_n[l].astype(F32), state_m[l].astype(F32))
        (C, n, m), mh = mlstm_chunk(carry0, mlstm_inputs(mq, mk, mv, mi, mf, b_igate[l], b_fgate[l]))
        xs = xs + merge(attn, mh, mo, ga, gm, *mw)
        xs = xs + 0.5 * swiglu(rmsnorm(xs, g_ffn2[l]), *ffn2)
        kS.append(k); vS.append(v); ikS.append(ki); CS.append(C); nS.append(n); mS.append(m)

    y_prompt = rmsnorm(xp, g_final)
    y_sample = rmsnorm(xs, g_final)
    return (y_prompt, y_sample,
            jnp.stack(kp), jnp.stack(vp), jnp.stack(ikp), jnp.stack(Cp), jnp.stack(nP), jnp.stack(mP),
            jnp.stack(kS), jnp.stack(vS), jnp.stack(ikS), jnp.stack(CS), jnp.stack(nS), jnp.stack(mS))
```

```python
import functools

import jax
import jax.numpy as jnp
import numpy as np
from jax import lax
from jax.experimental import pallas as pl
from jax.experimental.pallas import tpu as pltpu

F32 = jnp.float32
BF16 = jnp.bfloat16
I32 = jnp.int32

D_MODEL = 1024
D_FF = 2816
N_HEADS = 8
N_KV = 2
GROUP = N_HEADS // N_KV
HEAD_DIM = 64
IDX_HEADS = 4
IDX_DIM = 64
ROT_HALF = 8
ROPE_THETA = 500000.0
M_HEADS = 4
M_DIM = 128
CHUNK = 64
TOPK_MAX = 256
EPS = 1e-6
M_NEG = -1e30
PAST_LEN = 4096
SPLIT_SIZES = (N_HEADS * HEAD_DIM, N_KV * HEAD_DIM, N_KV * HEAD_DIM,
               IDX_HEADS * IDX_DIM, IDX_DIM, IDX_HEADS,
               M_HEADS * M_DIM, M_HEADS * M_DIM, M_HEADS * M_DIM, M_HEADS, M_HEADS, M_HEADS * M_DIM,
               D_MODEL, D_MODEL)

LANES = 128
INT_MIN = -2 ** 31
NEG_BIG = -1e30
VMEM_LIMIT = 48 * 1024 * 1024

OFF_Q, OFF_K, OFF_V, OFF_QI, OFF_G = 0, 512, 640, 768, 1024
OFF_MQ, OFF_MK, OFF_MV, OFF_MO, OFF_GA, OFF_GM = 1152, 1664, 2176, 2688, 3200, 4224
PROJ_COLS = 5248
G_WI, G_MI, G_MF = 64, 68, 72


def _params(sem):
    return pltpu.CompilerParams(dimension_semantics=sem, vmem_limit_bytes=VMEM_LIMIT)


def _dot(a, b):
    return jnp.dot(a, b, preferred_element_type=F32)


def _dot_nt(a, b):
    return lax.dot_general(a, b, (((1,), (1,)), ((), ())), preferred_element_type=F32)


def _tile(n, pref):
    while n % pref:
        pref //= 2
    return pref


def _rms(x, g):
    return x * lax.rsqrt(jnp.mean(x * x, axis=-1, keepdims=True) + EPS) * g


def _ffn_body(x_ref, g_ref, wg_ref, wu_ref, wd_ref, gf_ref, o_ref, h_scr, acc_scr, *, final_norm):
    f = pl.program_id(1)

    @pl.when(f == 0)
    def _():
        h_scr[...] = _rms(x_ref[...], g_ref[...]).astype(BF16)
        acc_scr[...] = jnp.zeros_like(acc_scr)

    h = h_scr[...]
    gate = _dot(h, wg_ref[...])
    up = _dot(h, wu_ref[...])
    act = (gate * jax.nn.sigmoid(gate) * up).astype(BF16)
    acc_scr[...] += _dot(act, wd_ref[...])

    @pl.when(f == pl.num_programs(1) - 1)
    def _():
        y = x_ref[...] + 0.5 * acc_scr[...]
        if final_norm:
            y = _rms(y, gf_ref[...])
        o_ref[...] = y


def _ffn(x, g, wg, wu, wd, g_final, *, final_norm, tm=512, tf=1408):
    n, d = x.shape
    dff = wg.shape[1]
    tm = _tile(n, tm)
    assert dff % tf == 0
    return pl.pallas_call(
        functools.partial(_ffn_body, final_norm=final_norm),
        grid=(n // tm, dff // tf),
        in_specs=[
            pl.BlockSpec((tm, d), lambda i, f: (i, 0)),
            pl.BlockSpec((1, d), lambda i, f: (0, 0)),
            pl.BlockSpec((d, tf), lambda i, f: (0, f)),
            pl.BlockSpec((d, tf), lambda i, f: (0, f)),
            pl.BlockSpec((tf, d), lambda i, f: (f, 0)),
            pl.BlockSpec((1, d), lambda i, f: (0, 0)),
        ],
        out_specs=pl.BlockSpec((tm, d), lambda i, f: (i, 0)),
        out_shape=jax.ShapeDtypeStruct((n, d), F32),
        scratch_shapes=[pltpu.VMEM((tm, d), BF16), pltpu.VMEM((tm, d), F32)],
        compiler_params=_params(("parallel", "arbitrary")),
        name="ffn",
    )(x, g.reshape(1, d), wg, wu, wd, g_final.reshape(1, d))


def _rope_tile(x, c, s1, s2):
    return x * c + pltpu.roll(x, ROT_HALF, 1) * s1 + pltpu.roll(x, LANES - ROT_HALF, 1) * s2


def _proj_body(x_ref, g_ref, w_ref, c_ref, s1_ref, s2_ref,
               q_ref, kg_ref, kf_ref, vt_ref, vf_ref, qi_ref, kib_ref, kif_ref, gt_ref,
               mq_ref, mk_ref, mv_ref, mo_ref, ga_ref, gm_ref):
    h = _rms(x_ref[...], g_ref[...]).astype(BF16)
    c, s1, s2 = c_ref[...], s1_ref[...], s2_ref[...]

    def seg(off, width):
        return _dot(h, w_ref[:, off:off + width])

    def rope(z):
        tiles = [_rope_tile(z[:, t * LANES:(t + 1) * LANES], c, s1, s2) for t in range(z.shape[1] // LANES)]
        return tiles[0] if len(tiles) == 1 else jnp.concatenate(tiles, axis=1)

    q = rope(seg(OFF_Q, N_HEADS * HEAD_DIM)) * (HEAD_DIM ** -0.5)
    for hd in range(N_HEADS):
        q_ref[hd] = q[:, hd * HEAD_DIM:(hd + 1) * HEAD_DIM].astype(BF16)

    k = rope(seg(OFF_K, N_KV * HEAD_DIM))
    kf_ref[...] = k
    for n in range(N_KV):
        kg_ref[n] = k[:, n * HEAD_DIM:(n + 1) * HEAD_DIM].astype(BF16)

    v = seg(OFF_V, N_KV * HEAD_DIM)
    vf_ref[...] = v
    vt = v.T
    ones = jnp.ones((HEAD_DIM, vt.shape[1]), BF16)
    for n in range(N_KV):
        vt_ref[n, 0:HEAD_DIM, :] = vt[n * HEAD_DIM:(n + 1) * HEAD_DIM, :].astype(BF16)
        vt_ref[n, HEAD_DIM:2 * HEAD_DIM, :] = ones

    qi = rope(seg(OFF_QI, IDX_HEADS * IDX_DIM)) * (IDX_DIM ** -0.5)
    for hd in range(IDX_HEADS):
        qi_ref[hd] = qi[:, hd * IDX_DIM:(hd + 1) * IDX_DIM].astype(BF16)

    gz = seg(OFF_G, LANES)
    lane = lax.broadcasted_iota(I32, gz.shape, 1)
    is_ki = lane < IDX_DIM
    gz = _rope_tile(gz, jnp.where(is_ki, c, 1.0), jnp.where(is_ki, s1, 0.0), jnp.where(is_ki, s2, 0.0))
    gt_ref[...] = gz
    kif_ref[...] = gz[:, 0:IDX_DIM]
    kib_ref[...] = gz[:, 0:IDX_DIM].astype(BF16)

    mq_ref[...] = seg(OFF_MQ, M_HEADS * M_DIM)
    mk_ref[...] = seg(OFF_MK, M_HEADS * M_DIM)
    mv_ref[...] = seg(OFF_MV, M_HEADS * M_DIM)
    mo_ref[...] = seg(OFF_MO, M_HEADS * M_DIM)
    ga_ref[...] = seg(OFF_GA, D_MODEL)
    gm_ref[...] = seg(OFF_GM, D_MODEL)


def _proj(x, g, w, cos_t, s1_t, s2_t, tab_block, *, tm=256):
    n, d = x.shape
    assert n % tm == 0
    hm = M_HEADS * M_DIM
    row = lambda i: (i, 0)
    tab = lambda i: (tab_block(i), 0)
    out_shapes = dict(
        q=((N_HEADS, n, HEAD_DIM), BF16, pl.BlockSpec((N_HEADS, tm, HEAD_DIM), lambda i: (0, i, 0))),
        kg=((N_KV, n, HEAD_DIM), BF16, pl.BlockSpec((N_KV, tm, HEAD_DIM), lambda i: (0, i, 0))),
        kf=((n, N_KV * HEAD_DIM), F32, pl.BlockSpec((tm, N_KV * HEAD_DIM), row)),
        vt=((N_KV, 2 * HEAD_DIM, n), BF16, pl.BlockSpec((N_KV, 2 * HEAD_DIM, tm), lambda i: (0, 0, i))),
        vf=((n, N_KV * HEAD_DIM), F32, pl.BlockSpec((tm, N_KV * HEAD_DIM), row)),
        qi=((IDX_HEADS, n, IDX_DIM), BF16, pl.BlockSpec((IDX_HEADS, tm, IDX_DIM), lambda i: (0, i, 0))),
        kib=((n, IDX_DIM), BF16, pl.BlockSpec((tm, IDX_DIM), row)),
        kif=((n, IDX_DIM), F32, pl.BlockSpec((tm, IDX_DIM), row)),
        gt=((n, LANES), F32, pl.BlockSpec((tm, LANES), row)),
        mq=((n, hm), F32, pl.BlockSpec((tm, hm), row)),
        mk=((n, hm), F32, pl.BlockSpec((tm, hm), row)),
        mv=((n, hm), F32, pl.BlockSpec((tm, hm), row)),
        mo=((n, hm), F32, pl.BlockSpec((tm, hm), row)),
        ga=((n, D_MODEL), F32, pl.BlockSpec((tm, D_MODEL), row)),
        gm=((n, D_MODEL), F32, pl.BlockSpec((tm, D_MODEL), row)),
    )
    names = list(out_shapes)
    outs = pl.pallas_call(
        _proj_body,
        grid=(n // tm,),
        in_specs=[
            pl.BlockSpec((tm, d), row),
            pl.BlockSpec((1, d), lambda i: (0, 0)),
            pl.BlockSpec((d, PROJ_COLS), lambda i: (0, 0)),
            pl.BlockSpec((tm, LANES), tab),
            pl.BlockSpec((tm, LANES), tab),
            pl.BlockSpec((tm, LANES), tab),
        ],
        out_specs=[out_shapes[k][2] for k in names],
        out_shape=[jax.ShapeDtypeStruct(out_shapes[k][0], out_shapes[k][1]) for k in names],
        compiler_params=_params(("parallel",)),
        name="proj",
    )(x, g.reshape(1, d), w, cos_t, s1_t, s2_t)
    return dict(zip(names, outs))


KB = 256


def _sortable(score):
    score = jnp.where(score == 0.0, 0.0, score)
    bits = lax.bitcast_convert_type(score, I32)
    return bits ^ ((bits >> 31) & 0x7FFFFFFF)


def _head_sum(x, tq):
    if tq == LANES:
        return x[:, 0:128] + x[:, 128:256] + x[:, 256:384] + x[:, 384:512]
    assert 4 * tq == LANES
    return x + pltpu.roll(x, tq, 1) + pltpu.roll(x, 2 * tq, 1) + pltpu.roll(x, 3 * tq, 1)


def _score_block(ki_blk, qis, w_row, limit_row, off, tq):
    lg = jnp.maximum(_dot_nt(ki_blk, qis), 0.0) * w_row
    key = _sortable(_head_sum(lg, tq))
    pos = off + lax.broadcasted_iota(I32, key.shape, 0)
    return jnp.where(pos < limit_row, key, INT_MIN)


def _count(keys_ref, n_blk, pred):
    def body(i, acc):
        t = keys_ref[pl.ds(pl.multiple_of(i * KB, KB), KB), :]
        ind = jnp.where(pred(t), 1, 0).astype(I32)
        return acc + jnp.sum(ind.reshape(KB // 8, 8, LANES), axis=0)

    acc = lax.fori_loop(0, n_blk, body, jnp.zeros((8, LANES), I32))
    return jnp.sum(acc.astype(F32), axis=0, keepdims=True)


def _select_threshold(keys_ref, n_blk, k_sel):
    kf = float(k_sel)
    c0 = _count(keys_ref, n_blk, lambda t: t >= 0)
    base = jnp.where(c0 >= kf, 0, INT_MIN).astype(I32)

    def bit_pass(p, base):
        cand = base | lax.shift_left(jnp.int32(1), 30 - p)
        c = _count(keys_ref, n_blk, lambda t: t >= cand)
        return jnp.where(c >= kf, cand, base)

    thr = lax.fori_loop(0, 31, bit_pass, base)
    c_gt = _count(keys_ref, n_blk, lambda t: t > thr)
    ties = jnp.where(thr == INT_MIN, 0.0, kf - c_gt)
    return thr, ties


def _attend(keys_ref, thr, ties, n_blk, k_blk, vt_blk, qs_ref, tri, acc_ref, m_ref, reps):
    acc_ref[...] = jnp.zeros_like(acc_ref)
    m_ref[...] = jnp.full(m_ref.shape, NEG_BIG, F32)

    def body(i, carry):
        off = pl.multiple_of(i * KB, KB)
        t = keys_ref[pl.ds(off, KB), :]
        eq = t == thr
        rank = _dot(tri, jnp.where(eq, 1.0, 0.0).astype(BF16)) + carry
        tie_bias = jnp.where(rank <= ties, 0.0, NEG_BIG)
        bias = jnp.where(t > thr, 0.0, jnp.where(eq, tie_bias, NEG_BIG))
        if reps > 1:
            bias = jnp.concatenate([bias] * reps, axis=1)
        for n in range(N_KV):
            s = _dot_nt(k_blk(n, off), qs_ref[n]) + bias
            m_old = m_ref[n]
            m_new = jnp.maximum(m_old, jnp.max(s, axis=0, keepdims=True))
            p = jnp.exp(s - m_new).astype(BF16)
            acc_ref[n] = acc_ref[n] * jnp.exp(m_old - m_new) + _dot(vt_blk(n, off), p)
            m_ref[n] = m_new
        return rank[KB - 1:KB, :]

    lax.fori_loop(0, n_blk, body, jnp.zeros((1, LANES), F32))


def _tri_const():
    r = lax.broadcasted_iota(I32, (KB, KB), 0)
    c = lax.broadcasted_iota(I32, (KB, KB), 1)
    return jnp.where(c <= r, 1.0, 0.0).astype(BF16)


def _dsa_prompt_body(qi_ref, q_ref, g_ref, ki_ref, kg_ref, vt_ref, o_ref,
                     keys_ref, qs_ref, acc_ref, m_ref, *, tq, k_sel):
    j = pl.program_id(1)
    n_keys = (j + 1) * tq
    n_blk = (n_keys + KB - 1) // KB

    qis = qi_ref[...].reshape(IDX_HEADS * tq, IDX_DIM)
    qs_ref[0] = q_ref[0:GROUP].reshape(GROUP * tq, HEAD_DIM)
    qs_ref[1] = q_ref[GROUP:2 * GROUP].reshape(GROUP * tq, HEAD_DIM)
    gt = g_ref[...].T
    w_row = jnp.concatenate([gt[G_WI + h:G_WI + h + 1, :] for h in range(IDX_HEADS)], axis=1) * (IDX_HEADS ** -0.5)
    qpos = j * tq + lax.broadcasted_iota(I32, (1, LANES), 1)
    limit_row = (qpos // CHUNK + 1) * CHUNK

    def score(i, _):
        off = pl.multiple_of(i * KB, KB)
        keys_ref[pl.ds(off, KB), :] = _score_block(ki_ref[pl.ds(off, KB), :], qis, w_row, limit_row, off, tq)
        return 0

    lax.fori_loop(0, n_blk, score, 0)
    thr, ties = _select_threshold(keys_ref, n_blk, k_sel)
    _attend(keys_ref, thr, ties, n_blk,
            lambda n, off: kg_ref[n, pl.ds(off, KB), :],
            lambda n, off: vt_ref[n, :, pl.ds(off, KB)],
            qs_ref, _tri_const(), acc_ref, m_ref, reps=GROUP)

    for n in range(N_KV):
        a = acc_ref[n]
        o_t = a[0:HEAD_DIM, :] / a[HEAD_DIM:HEAD_DIM + 1, :]
        for pair in range(GROUP // 2):
            two = jnp.concatenate([o_t[:, (2 * pair) * tq:(2 * pair + 1) * tq],
                                   o_t[:, (2 * pair + 1) * tq:(2 * pair + 2) * tq]], axis=0)
            col = (n * GROUP + 2 * pair) * HEAD_DIM
            o_ref[:, col:col + 2 * HEAD_DIM] = two.T


def _dsa_prompt(p, batch, seq, k_sel, *, tq=LANES):
    nq = seq // tq
    return pl.pallas_call(
        functools.partial(_dsa_prompt_body, tq=tq, k_sel=k_sel),
        grid=(batch, nq),
        in_specs=[
            pl.BlockSpec((IDX_HEADS, tq, IDX_DIM), lambda b, j: (0, b * nq + j, 0)),
            pl.BlockSpec((N_HEADS, tq, HEAD_DIM), lambda b, j: (0, b * nq + j, 0)),
            pl.BlockSpec((tq, LANES), lambda b, j: (b * nq + j, 0)),
            pl.BlockSpec((seq, IDX_DIM), lambda b, j: (b, 0)),
            pl.BlockSpec((N_KV, seq, HEAD_DIM), lambda b, j: (0, b, 0)),
            pl.BlockSpec((N_KV, 2 * HEAD_DIM, seq), lambda b, j: (0, 0, b)),
        ],
        out_specs=pl.BlockSpec((tq, N_HEADS * HEAD_DIM), lambda b, j: (b * nq + j, 0)),
        out_shape=jax.ShapeDtypeStruct((batch * seq, N_HEADS * HEAD_DIM), F32),
        scratch_shapes=[
            pltpu.VMEM((seq, LANES), I32),
            pltpu.VMEM((N_KV, GROUP * tq, HEAD_DIM), BF16),
            pltpu.VMEM((N_KV, 2 * HEAD_DIM, GROUP * tq), F32),
            pltpu.VMEM((N_KV, 1, GROUP * tq), F32),
        ],
        compiler_params=_params(("arbitrary", "arbitrary")),
        name="dsa_prompt",
    )(p["qi"], p["q"], p["gt"], p["kib"], p["kg"], p["vt"])


def _dsa_sample_body(qi_ref, q_ref, g_ref, kin_ref, kn_ref, vn_ref, kc_ref, vc_ref, ic_ref, o_ref,
                     keys_ref, kib_scr, kg_scr, vt_scr, qs_ref, acc_ref, m_ref, *, tq, past, k_sel):
    l_pad = keys_ref.shape[0]
    n_blk = l_pad // KB
    n_keys = past + tq

    fill = 512
    for c in range(past // fill):
        rows = slice(c * fill, (c + 1) * fill)
        kc = kc_ref[rows, :]
        vt = vc_ref[rows, :].T
        kib_scr[rows, :] = ic_ref[rows, :].astype(BF16)
        for n in range(N_KV):
            kg_scr[n, rows, :] = kc[:, n * HEAD_DIM:(n + 1) * HEAD_DIM].astype(BF16)
            vt_scr[n, 0:HEAD_DIM, rows] = vt[n * HEAD_DIM:(n + 1) * HEAD_DIM, :].astype(BF16)
    tail = l_pad - past
    kn = jnp.concatenate([kn_ref[...], jnp.zeros((tail - tq, N_KV * HEAD_DIM), F32)], axis=0)
    vn = jnp.concatenate([vn_ref[...], jnp.zeros((tail - tq, N_KV * HEAD_DIM), F32)], axis=0).T
    kib_scr[past:l_pad, :] = jnp.concatenate(
        [kin_ref[...], jnp.zeros((tail - tq, IDX_DIM), F32)], axis=0).astype(BF16)
    for n in range(N_KV):
        kg_scr[n, past:l_pad, :] = kn[:, n * HEAD_DIM:(n + 1) * HEAD_DIM].astype(BF16)
        vt_scr[n, 0:HEAD_DIM, past:l_pad] = vn[n * HEAD_DIM:(n + 1) * HEAD_DIM, :].astype(BF16)
        vt_scr[n, HEAD_DIM:2 * HEAD_DIM, :] = jnp.ones((HEAD_DIM, l_pad), BF16)

    qis = qi_ref[...].reshape(IDX_HEADS * tq, IDX_DIM)
    qs_ref[0] = q_ref[0:GROUP].reshape(GROUP * tq, HEAD_DIM)
    qs_ref[1] = q_ref[GROUP:2 * GROUP].reshape(GROUP * tq, HEAD_DIM)
    eye = jnp.where(lax.broadcasted_iota(I32, (LANES, LANES), 0) == lax.broadcasted_iota(I32, (LANES, LANES), 1),
                    1.0, 0.0)
    gt = lax.dot_general(eye, g_ref[...], (((1,), (1,)), ((), ())),
                         precision=lax.Precision.HIGHEST, preferred_element_type=F32)
    w_row = jnp.concatenate([gt[G_WI + h:G_WI + h + 1, :] for h in range(IDX_HEADS)], axis=1) * (IDX_HEADS ** -0.5)
    limit_row = jnp.full((1, LANES), n_keys, I32)

    for i in range(n_blk):
        off = i * KB
        keys_ref[off:off + KB, :] = _score_block(kib_scr[off:off + KB, :], qis, w_row, limit_row, off, tq)
    thr, ties = _select_threshold(keys_ref, n_blk, k_sel)
    _attend(keys_ref, thr, ties, n_blk,
            lambda n, off: kg_scr[n, pl.ds(off, KB), :],
            lambda n, off: vt_scr[n, :, pl.ds(off, KB)],
            qs_ref, _tri_const(), acc_ref, m_ref, reps=1)

    for n in range(N_KV):
        a = acc_ref[n]
        o = (a[0:HEAD_DIM, :] / a[HEAD_DIM:HEAD_DIM + 1, :]).T
        for pair in range(GROUP // 2):
            col = (n * GROUP + 2 * pair) * HEAD_DIM
            o_ref[:, col:col + 2 * HEAD_DIM] = jnp.concatenate(
                [o[(2 * pair) * tq:(2 * pair + 1) * tq, :], o[(2 * pair + 1) * tq:(2 * pair + 2) * tq, :]], axis=1)


def _dsa_sample(p, row0, streams, tq, cache_k, cache_v, cache_ik, k_sel):
    past = cache_k.shape[1]
    l_pad = ((past + tq + KB - 1) // KB) * KB
    b0 = row0 // tq
    return pl.pallas_call(
        functools.partial(_dsa_sample_body, tq=tq, past=past, k_sel=k_sel),
        grid=(streams,),
        in_specs=[
            pl.BlockSpec((IDX_HEADS, tq, IDX_DIM), lambda s: (0, b0 + s, 0)),
            pl.BlockSpec((N_HEADS, tq, HEAD_DIM), lambda s: (0, b0 + s, 0)),
            pl.BlockSpec((tq, LANES), lambda s: (b0 + s, 0)),
            pl.BlockSpec((tq, IDX_DIM), lambda s: (b0 + s, 0)),
            pl.BlockSpec((tq, N_KV * HEAD_DIM), lambda s: (b0 + s, 0)),
            pl.BlockSpec((tq, N_KV * HEAD_DIM), lambda s: (b0 + s, 0)),
            pl.BlockSpec((None, past, N_KV * HEAD_DIM), lambda s: (s, 0, 0)),
            pl.BlockSpec((None, past, N_KV * HEAD_DIM), lambda s: (s, 0, 0)),
            pl.BlockSpec((None, past, IDX_DIM), lambda s: (s, 0, 0)),
        ],
        out_specs=pl.BlockSpec((tq, N_HEADS * HEAD_DIM), lambda s: (s, 0)),
        out_shape=jax.ShapeDtypeStruct((streams * tq, N_HEADS * HEAD_DIM), F32),
        scratch_shapes=[
            pltpu.VMEM((l_pad, LANES), I32),
            pltpu.VMEM((l_pad, IDX_DIM), BF16),
            pltpu.VMEM((N_KV, l_pad, HEAD_DIM), BF16),
            pltpu.VMEM((N_KV, 2 * HEAD_DIM, l_pad), BF16),
            pltpu.VMEM((N_KV, GROUP * tq, HEAD_DIM), BF16),
            pltpu.VMEM((N_KV, 2 * HEAD_DIM, GROUP * tq), F32),
            pltpu.VMEM((N_KV, 1, GROUP * tq), F32),
        ],
        compiler_params=_params(("arbitrary",)),
        name="dsa_sample",
    )(p["qi"], p["q"], p["gt"], p["kif"], p["kf"], p["vf"], cache_k, cache_v, cache_ik)


def _log_sigmoid(x):
    return jnp.minimum(x, 0.0) - jnp.log1p(jnp.exp(-jnp.abs(x)))


def _mlstm_body(mq_ref, mk_ref, mv_ref, g_ref, bias_ref, c0_ref, n0_ref, m0_ref,
                h_ref, c_ref, n_ref, m_ref, *, t):
    @pl.when(pl.program_id(1) == 0)
    def _():
        c_ref[...] = c0_ref[...]
        n_ref[...] = n0_ref[...]
        m_ref[...] = m0_ref[...]

    gb = g_ref[...] + bias_ref[0:1, :]
    eye = jnp.where(lax.broadcasted_iota(I32, (LANES, LANES), 0) == lax.broadcasted_iota(I32, (LANES, LANES), 1),
                    1.0, 0.0)
    gbt = lax.dot_general(eye, gb, (((1,), (1,)), ((), ())),
                          precision=lax.Precision.HIGHEST, preferred_element_type=F32)
    r = lax.broadcasted_iota(I32, (t, t), 0)
    c = lax.broadcasted_iota(I32, (t, t), 1)
    causal = c <= r
    for h in range(M_HEADS):
        ig_row = gbt[G_MI + h:G_MI + h + 1, :]
        lf_row = _log_sigmoid(gbt[G_MF + h:G_MF + h + 1, :])
        ig_col = gb[:, G_MI + h:G_MI + h + 1]
        lf_col = _log_sigmoid(gb[:, G_MF + h:G_MF + h + 1])
        b_col = jnp.sum(jnp.where(causal, lf_row, 0.0), axis=1, keepdims=True)
        b_row = jnp.sum(jnp.where(r <= c, lf_col, 0.0), axis=0, keepdims=True)
        b_last = b_col[t - 1:t, :]
        m_prev = m_ref[0, h:h + 1, 0:1]
        d = jnp.where(causal, b_col - b_row + ig_row, -jnp.inf)
        m_inter = b_col + m_prev
        m_t = jnp.maximum(m_inter, jnp.max(d, axis=1, keepdims=True))
        sl = slice(h * M_DIM, (h + 1) * M_DIM)
        q = mq_ref[:, sl]
        k = mk_ref[:, sl] * (M_DIM ** -0.5)
        v = mv_ref[:, sl]
        qb, kb = q.astype(BF16), k.astype(BF16)
        sc = _dot_nt(qb, kb) * jnp.exp(d - m_t)
        decay = jnp.exp(m_inter - m_t)
        c_old = c_ref[0, h]
        n_old = n_ref[0, h:h + 1, :]
        num = _dot(sc.astype(BF16), v.astype(BF16)) + decay * _dot_nt(qb, c_old.astype(BF16))
        den = jnp.sum(sc, axis=1, keepdims=True) + decay * jnp.sum(q * n_old, axis=1, keepdims=True)
        h_ref[:, sl] = num / jnp.maximum(jnp.abs(den), jnp.exp(-m_t))
        m_new = m_t[t - 1:t, :]
        g_col = jnp.exp(b_last - b_col + ig_col - m_new)
        dec_end = jnp.exp(b_last + m_prev - m_new)
        vg_t = (v * g_col).T.astype(BF16)
        c_ref[0, h] = dec_end * c_old + _dot(vg_t, kb)
        n_ref[0, h:h + 1, :] = dec_end * n_old + jnp.sum(g_col * k, axis=0, keepdims=True)
        m_ref[0, h:h + 1, :] = jnp.broadcast_to(m_new, (1, LANES))


def _mlstm(p, row0, batch, seq, t, bias, c0, n0, m0):
    nc = seq // t
    b0 = row0 // t
    hm = M_HEADS * M_DIM
    tok = lambda b, c: (b0 + b * nc + c, 0)
    st4 = lambda b, c: (b, 0, 0, 0)
    st3 = lambda b, c: (b, 0, 0)
    return pl.pallas_call(
        functools.partial(_mlstm_body, t=t),
        grid=(batch, nc),
        in_specs=[
            pl.BlockSpec((t, hm), tok), pl.BlockSpec((t, hm), tok), pl.BlockSpec((t, hm), tok),
            pl.BlockSpec((t, LANES), tok),
            pl.BlockSpec((8, LANES), lambda b, c: (0, 0)),
            pl.BlockSpec((1, M_HEADS, M_DIM, M_DIM), st4),
            pl.BlockSpec((1, M_HEADS, M_DIM), st3),
            pl.BlockSpec((1, M_HEADS, LANES), st3),
        ],
        out_specs=[
            pl.BlockSpec((t, hm), lambda b, c: (b * nc + c, 0)),
            pl.BlockSpec((1, M_HEADS, M_DIM, M_DIM), st4),
            pl.BlockSpec((1, M_HEADS, M_DIM), st3),
            pl.BlockSpec((1, M_HEADS, LANES), st3),
        ],
        out_shape=[
            jax.ShapeDtypeStruct((batch * seq, hm), F32),
            jax.ShapeDtypeStruct((batch, M_HEADS, M_DIM, M_DIM), F32),
            jax.ShapeDtypeStruct((batch, M_HEADS, M_DIM), F32),
            jax.ShapeDtypeStruct((batch, M_HEADS, LANES), F32),
        ],
        compiler_params=_params(("arbitrary", "arbitrary")),
        name="mlstm",
    )(p["mq"], p["mk"], p["mv"], p["gt"], bias, c0, n0, m0)


def _merge_body(x_ref, attn_ref, mh_ref, mo_ref, ga_ref, gm_ref, gml_ref, wap_ref, wmp_ref, wo_ref, o_ref):
    a = _dot(attn_ref[...].astype(BF16), wap_ref[...])
    parts = []
    for h in range(M_HEADS):
        sl = slice(h * M_DIM, (h + 1) * M_DIM)
        parts.append(_rms(mh_ref[:, sl], gml_ref[:, sl]))
    hn = jnp.concatenate(parts, axis=1)
    b = _dot((jax.nn.sigmoid(mo_ref[...]) * hn).astype(BF16), wmp_ref[...])
    mix = jax.nn.sigmoid(ga_ref[...]) * a + jax.nn.sigmoid(gm_ref[...]) * b
    o_ref[...] = x_ref[...] + _dot(mix.astype(BF16), wo_ref[...])


def _merge(x, attn, mh, p, g_ml, w_ap, w_mp, w_o, *, tm=256):
    n, d = x.shape
    tm = _tile(n, tm)
    ha, hm = N_HEADS * HEAD_DIM, M_HEADS * M_DIM
    row = lambda i: (i, 0)
    fix = lambda i: (0, 0)
    return pl.pallas_call(
        _merge_body,
        grid=(n // tm,),
        in_specs=[
            pl.BlockSpec((tm, d), row), pl.BlockSpec((tm, ha), row), pl.BlockSpec((tm, hm), row),
            pl.BlockSpec((tm, hm), row), pl.BlockSpec((tm, d), row), pl.BlockSpec((tm, d), row),
            pl.BlockSpec((1, hm), fix), pl.BlockSpec((ha, d), fix), pl.BlockSpec((hm, d), fix),
            pl.BlockSpec((d, d), fix),
        ],
        out_specs=pl.BlockSpec((tm, d), row),
        out_shape=jax.ShapeDtypeStruct((n, d), F32),
        compiler_params=_params(("parallel",)),
        name="merge",
    )(x, attn, mh, p["mo"], p["ga"], p["gm"], g_ml.reshape(1, hm), w_ap, w_mp, w_o)


def _rearrange_w_in(w):
    cuts = [int(c) for c in np.cumsum(SPLIT_SIZES)[:-1]]
    q, k, v, qi, ki, wi, mq, mk, mv, mi, mf, mo, ga, gm = jnp.split(w, cuts, axis=1)
    pad = jnp.zeros((w.shape[0], LANES - IDX_DIM - 3 * IDX_HEADS), w.dtype)
    return jnp.concatenate([q, k, v, qi, ki, wi, mi, mf, pad, mq, mk, mv, mo, ga, gm], axis=1).astype(BF16)


def _rope_tables(pos):
    freqs = ROPE_THETA ** (-jnp.arange(ROT_HALF, dtype=F32) / ROT_HALF)
    ang = pos.astype(F32)[:, None] * freqs[None, :]
    cos, sin = jnp.cos(ang), jnp.sin(ang)
    n = pos.shape[0]
    one = jnp.ones((n, HEAD_DIM - 2 * ROT_HALF), F32)
    zero8 = jnp.zeros((n, ROT_HALF), F32)
    zero = jnp.zeros((n, HEAD_DIM - 2 * ROT_HALF), F32)
    two = lambda a: jnp.concatenate([a, a], axis=1)
    return (two(jnp.concatenate([cos, cos, one], axis=1)),
            two(jnp.concatenate([zero8, sin, zero], axis=1)),
            two(jnp.concatenate([-sin, zero8, zero], axis=1)))


def kernel(x_prompt, x_sample, cache_k, cache_v, cache_idx_k, state_C, state_n, state_m,
           g_ffn1, w_ffn1_gate, w_ffn1_up, w_ffn1_down, g_mix, w_in, b_igate, b_fgate,
           g_mlstm, w_attn_proj, w_mlstm_proj, w_out, g_ffn2, w_ffn2_gate, w_ffn2_up,
           w_ffn2_down, g_final):
    batch, seq, d = x_prompt.shape
    streams, t_s, _ = x_sample.shape
    depth = w_in.shape[0]
    past = cache_k.shape[2]
    n_p, n_s = batch * seq, streams * t_s
    proj_tm = 256
    t_prompt = 256

    x = jnp.concatenate([x_prompt.reshape(n_p, d), x_sample.reshape(n_s, d)], axis=0)

    pos = jnp.concatenate([jnp.arange(seq, dtype=I32),
                           past + (jnp.arange(proj_tm, dtype=I32) % t_s)])
    cos_t, s1_t, s2_t = _rope_tables(pos)
    tiles_per_seq = seq // proj_tm
    n_prompt_tiles = n_p // proj_tm
    tab_block = lambda i: jnp.where(i < n_prompt_tiles, i % tiles_per_seq, tiles_per_seq)

    k_sel_p = min(TOPK_MAX, seq // 4)
    k_sel_s = min(TOPK_MAX, (past + t_s) // 4)

    kp, vp, ikp, cp, np_, mp = [], [], [], [], [], []
    ks, vs, iks, cs, ns, ms = [], [], [], [], [], []
    for l in range(depth):
        last = l == depth - 1
        x = _ffn(x, g_ffn1[l], w_ffn1_gate[l].astype(BF16), w_ffn1_up[l].astype(BF16),
                 w_ffn1_down[l].astype(BF16), g_final, final_norm=False)
        p = _proj(x, g_mix[l], _rearrange_w_in(w_in[l]), cos_t, s1_t, s2_t, tab_block, tm=proj_tm)

        attn_p = _dsa_prompt(p, batch, seq, k_sel_p)
        attn_s = _dsa_sample(p, n_p, streams, t_s,
                             cache_k[l].reshape(streams, past, N_KV * HEAD_DIM),
                             cache_v[l].reshape(streams, past, N_KV * HEAD_DIM),
                             cache_idx_k[l], k_sel_s)

        bias = jnp.zeros((8, LANES), F32)
        bias = bias.at[0, G_MI:G_MI + M_HEADS].set(b_igate[l]).at[0, G_MF:G_MF + M_HEADS].set(b_fgate[l])
        mh_p, c_p, n_pr, m_p = _mlstm(
            p, 0, batch, seq, t_prompt, bias,
            jnp.zeros((batch, M_HEADS, M_DIM, M_DIM), F32), jnp.zeros((batch, M_HEADS, M_DIM), F32),
            jnp.full((batch, M_HEADS, LANES), M_NEG, F32))
        mh_s, c_s, n_sm, m_s = _mlstm(
            p, n_p, streams, t_s, t_s, bias, state_C[l], state_n[l],
            jnp.broadcast_to(state_m[l][:, :, None], (streams, M_HEADS, LANES)))

        x = _merge(x, jnp.concatenate([attn_p, attn_s], axis=0), jnp.concatenate([mh_p, mh_s], axis=0), p,
                   g_mlstm[l], w_attn_proj[l].astype(BF16), w_mlstm_proj[l].astype(BF16), w_out[l].astype(BF16))
        x = _ffn(x, g_ffn2[l], w_ffn2_gate[l].astype(BF16), w_ffn2_up[l].astype(BF16),
                 w_ffn2_down[l].astype(BF16), g_final, final_norm=last)

        kp.append(p["kf"][:n_p].reshape(batch, seq, N_KV, HEAD_DIM))
        vp.append(p["vf"][:n_p].reshape(batch, seq, N_KV, HEAD_DIM))
        ikp.append(p["kif"][:n_p].reshape(batch, seq, IDX_DIM))
        cp.append(c_p); np_.append(n_pr); mp.append(m_p[:, :, 0])
        ks.append(p["kf"][n_p:].reshape(streams, t_s, N_KV, HEAD_DIM))
        vs.append(p["vf"][n_p:].reshape(streams, t_s, N_KV, HEAD_DIM))
        iks.append(p["kif"][n_p:].reshape(streams, t_s, IDX_DIM))
        cs.append(c_s); ns.append(n_sm); ms.append(m_s[:, :, 0])

    st = jnp.stack
    return (x[:n_p].reshape(batch, seq, d), x[n_p:].reshape(streams, t_s, d),
            st(kp), st(vp), st(ikp), st(cp), st(np_), st(mp),
            st(ks), st(vs), st(iks), st(cs), st(ns), st(ms))
```

```python
import functools

import jax
import jax.numpy as jnp
import numpy as np
from jax import lax
from jax.experimental import pallas as pl
from jax.experimental.pallas import tpu as pltpu

F32 = jnp.float32
BF16 = jnp.bfloat16
I32 = jnp.int32

D_MODEL = 1024
D_FF = 2816
N_HEADS = 8
N_KV = 2
GROUP = N_HEADS // N_KV
HEAD_DIM = 64
IDX_HEADS = 4
IDX_DIM = 64
ROT_HALF = 8
ROPE_THETA = 500000.0
M_HEADS = 4
M_DIM = 128
CHUNK = 64
TOPK_MAX = 256
EPS = 1e-6
M_NEG = -1e30
PAST_LEN = 4096
SPLIT_SIZES = (N_HEADS * HEAD_DIM, N_KV * HEAD_DIM, N_KV * HEAD_DIM,
               IDX_HEADS * IDX_DIM, IDX_DIM, IDX_HEADS,
               M_HEADS * M_DIM, M_HEADS * M_DIM, M_HEADS * M_DIM, M_HEADS, M_HEADS, M_HEADS * M_DIM,
               D_MODEL, D_MODEL)

LANES = 128
INT_MIN = -2 ** 31
NEG_BIG = -1e30
VMEM_LIMIT = 48 * 1024 * 1024

OFF_Q, OFF_K, OFF_V, OFF_QI, OFF_G = 0, 512, 640, 768, 1024
OFF_MQ, OFF_MK, OFF_MV, OFF_MO, OFF_GA, OFF_GM = 1152, 1664, 2176, 2688, 3200, 4224
PROJ_COLS = 5248
G_WI, G_MI, G_MF = 64, 68, 72


def _params(sem):
    return pltpu.CompilerParams(dimension_semantics=sem, vmem_limit_bytes=VMEM_LIMIT)


def _dot(a, b):
    return jnp.dot(a, b, preferred_element_type=F32)


def _dot_nt(a, b):
    return lax.dot_general(a, b, (((1,), (1,)), ((), ())), preferred_element_type=F32)


def _tile(n, pref):
    while n % pref:
        pref //= 2
    return pref


def _rms(x, g):
    return x * lax.rsqrt(jnp.mean(x * x, axis=-1, keepdims=True) + EPS) * g


def _ffn_body(x_ref, g_ref, wg_ref, wu_ref, wd_ref, gf_ref, o_ref, h_scr, acc_scr, *, final_norm):
    f = pl.program_id(1)

    @pl.when(f == 0)
    def _():
        h_scr[...] = _rms(x_ref[...], g_ref[...]).astype(BF16)
        acc_scr[...] = jnp.zeros_like(acc_scr)

    h = h_scr[...]
    gate = _dot(h, wg_ref[...])
    up = _dot(h, wu_ref[...])
    act = (gate * jax.nn.sigmoid(gate) * up).astype(BF16)
    acc_scr[...] += _dot(act, wd_ref[...])

    @pl.when(f == pl.num_programs(1) - 1)
    def _():
        y = x_ref[...] + 0.5 * acc_scr[...]
        if final_norm:
            y = _rms(y, gf_ref[...])
        o_ref[...] = y


def _ffn(x, g, wg, wu, wd, g_final, *, final_norm, tm=512, tf=1408):
    n, d = x.shape
    dff = wg.shape[1]
    tm = _tile(n, tm)
    assert dff % tf == 0
    return pl.pallas_call(
        functools.partial(_ffn_body, final_norm=final_norm),
        grid=(n // tm, dff // tf),
        in_specs=[
            pl.BlockSpec((tm, d), lambda i, f: (i, 0)),
            pl.BlockSpec((1, d), lambda i, f: (0, 0)),
            pl.BlockSpec((d, tf), lambda i, f: (0, f)),
            pl.BlockSpec((d, tf), lambda i, f: (0, f)),
            pl.BlockSpec((tf, d), lambda i, f: (f, 0)),
            pl.BlockSpec((1, d), lambda i, f: (0, 0)),
        ],
        out_specs=pl.BlockSpec((tm, d), lambda i, f: (i, 0)),
        out_shape=jax.ShapeDtypeStruct((n, d), F32),
        scratch_shapes=[pltpu.VMEM((tm, d), BF16), pltpu.VMEM((tm, d), F32)],
        compiler_params=_params(("parallel", "arbitrary")),
        name="ffn",
    )(x, g.reshape(1, d), wg, wu, wd, g_final.reshape(1, d))


def _rope_tile(x, c, s1, s2):
    return x * c + pltpu.roll(x, ROT_HALF, 1) * s1 + pltpu.roll(x, LANES - ROT_HALF, 1) * s2


def _proj_body(x_ref, g_ref, w_ref, c_ref, s1_ref, s2_ref,
               q_ref, kg_ref, kf_ref, vt_ref, vf_ref, qi_ref, kib_ref, kif_ref, gt_ref,
               mq_ref, mk_ref, mv_ref, mo_ref, ga_ref, gm_ref):
    h = _rms(x_ref[...], g_ref[...]).astype(BF16)
    c, s1, s2 = c_ref[...], s1_ref[...], s2_ref[...]

    def seg(off, width):
        return _dot(h, w_ref[:, off:off + width])

    def rope(z):
        tiles = [_rope_tile(z[:, t * LANES:(t + 1) * LANES], c, s1, s2) for t in range(z.shape[1] // LANES)]
        return tiles[0] if len(tiles) == 1 else jnp.concatenate(tiles, axis=1)

    q = rope(seg(OFF_Q, N_HEADS * HEAD_DIM)) * (HEAD_DIM ** -0.5)
    for hd in range(N_HEADS):
        q_ref[hd] = q[:, hd * HEAD_DIM:(hd + 1) * HEAD_DIM].astype(BF16)

    k = rope(seg(OFF_K, N_KV * HEAD_DIM))
    kf_ref[...] = k
    for n in range(N_KV):
        kg_ref[n] = k[:, n * HEAD_DIM:(n + 1) * HEAD_DIM].astype(BF16)

    v = seg(OFF_V, N_KV * HEAD_DIM)
    vf_ref[...] = v
    vt = v.T
    ones = jnp.ones((HEAD_DIM, vt.shape[1]), BF16)
    for n in range(N_KV):
        vt_ref[n, 0:HEAD_DIM, :] = vt[n * HEAD_DIM:(n + 1) * HEAD_DIM, :].astype(BF16)
        vt_ref[n, HEAD_DIM:2 * HEAD_DIM, :] = ones

    qi = rope(seg(OFF_QI, IDX_HEADS * IDX_DIM)) * (IDX_DIM ** -0.5)
    for hd in range(IDX_HEADS):
        qi_ref[hd] = qi[:, hd * IDX_DIM:(hd + 1) * IDX_DIM].astype(BF16)

    gz = seg(OFF_G, LANES)
    lane = lax.broadcasted_iota(I32, gz.shape, 1)
    is_ki = lane < IDX_DIM
    gz = _rope_tile(gz, jnp.where(is_ki, c, 1.0), jnp.where(is_ki, s1, 0.0), jnp.where(is_ki, s2, 0.0))
    gt_ref[...] = gz
    kif_ref[...] = gz[:, 0:IDX_DIM]
    kib_ref[...] = gz[:, 0:IDX_DIM].astype(BF16)

    mq_ref[...] = seg(OFF_MQ, M_HEADS * M_DIM)
    mk_ref[...] = seg(OFF_MK, M_HEADS * M_DIM)
    mv_ref[...] = seg(OFF_MV, M_HEADS * M_DIM)
    mo_ref[...] = seg(OFF_MO, M_HEADS * M_DIM)
    ga_ref[...] = seg(OFF_GA, D_MODEL)
    gm_ref[...] = seg(OFF_GM, D_MODEL)


def _proj(x, g, w, cos_t, s1_t, s2_t, tab_block, *, tm=256):
    n, d = x.shape
    assert n % tm == 0
    hm = M_HEADS * M_DIM
    row = lambda i: (i, 0)
    tab = lambda i: (tab_block(i), 0)
    out_shapes = dict(
        q=((N_HEADS, n, HEAD_DIM), BF16, pl.BlockSpec((N_HEADS, tm, HEAD_DIM), lambda i: (0, i, 0))),
        kg=((N_KV, n, HEAD_DIM), BF16, pl.BlockSpec((N_KV, tm, HEAD_DIM), lambda i: (0, i, 0))),
        kf=((n, N_KV * HEAD_DIM), F32, pl.BlockSpec((tm, N_KV * HEAD_DIM), row)),
        vt=((N_KV, 2 * HEAD_DIM, n), BF16, pl.BlockSpec((N_KV, 2 * HEAD_DIM, tm), lambda i: (0, 0, i))),
        vf=((n, N_KV * HEAD_DIM), F32, pl.BlockSpec((tm, N_KV * HEAD_DIM), row)),
        qi=((IDX_HEADS, n, IDX_DIM), BF16, pl.BlockSpec((IDX_HEADS, tm, IDX_DIM), lambda i: (0, i, 0))),
        kib=((n, IDX_DIM), BF16, pl.BlockSpec((tm, IDX_DIM), row)),
        kif=((n, IDX_DIM), F32, pl.BlockSpec((tm, IDX_DIM), row)),
        gt=((n, LANES), F32, pl.BlockSpec((tm, LANES), row)),
        mq=((n, hm), F32, pl.BlockSpec((tm, hm), row)),
        mk=((n, hm), F32, pl.BlockSpec((tm, hm), row)),
        mv=((n, hm), F32, pl.BlockSpec((tm, hm), row)),
        mo=((n, hm), F32, pl.BlockSpec((tm, hm), row)),
        ga=((n, D_MODEL), F32, pl.BlockSpec((tm, D_MODEL), row)),
        gm=((n, D_MODEL), F32, pl.BlockSpec((tm, D_MODEL), row)),
    )
    names = list(out_shapes)
    outs = pl.pallas_call(
        _proj_body,
        grid=(n // tm,),
        in_specs=[
            pl.BlockSpec((tm, d), row),
            pl.BlockSpec((1, d), lambda i: (0, 0)),
            pl.BlockSpec((d, PROJ_COLS), lambda i: (0, 0)),
            pl.BlockSpec((tm, LANES), tab),
            pl.BlockSpec((tm, LANES), tab),
            pl.BlockSpec((tm, LANES), tab),
        ],
        out_specs=[out_shapes[k][2] for k in names],
        out_shape=[jax.ShapeDtypeStruct(out_shapes[k][0], out_shapes[k][1]) for k in names],
        compiler_params=_params(("parallel",)),
        name="proj",
    )(x, g.reshape(1, d), w, cos_t, s1_t, s2_t)
    return dict(zip(names, outs))


TRIP = 512
SUB = 256
PACK = 16
I16 = jnp.int16
I16_MIN = -2 ** 15


def _sortable(score):
    score = jnp.where(score == 0.0, 0.0, score)
    bits = lax.bitcast_convert_type(score, I32)
    return bits ^ ((bits >> 31) & 0x7FFFFFFF)


def _head_sum(x, tq):
    if tq == LANES:
        return x[:, 0:128] + x[:, 128:256] + x[:, 256:384] + x[:, 384:512]
    assert 4 * tq == LANES
    return x + pltpu.roll(x, tq, 1) + pltpu.roll(x, 2 * tq, 1) + pltpu.roll(x, 3 * tq, 1)


def _score_trip(ki_blk, qis, w_row, limit_row, off, tq):
    lg = jnp.maximum(_dot_nt(ki_blk, qis), 0.0) * w_row
    key = _sortable(_head_sum(lg, tq))
    pos = off + lax.broadcasted_iota(I32, key.shape, 0)
    key = jnp.where(pos < limit_row, key, INT_MIN)
    return (key >> 16).astype(I16), ((key & 0xFFFF) + I16_MIN).astype(I16)


def _row16(row):
    return jnp.broadcast_to(row, (PACK, LANES)).astype(I16)


def _planes(ref, off, rows):
    return ref[pl.ds(off, rows), :].reshape(rows // PACK, PACK, LANES)


def _count16(ref, n_trips, cand, strict):
    c16 = _row16(cand)

    def body(i, accs):
        t = _planes(ref, pl.multiple_of(i * TRIP, TRIP), TRIP)
        ind = jnp.where(t > c16 if strict else t >= c16, jnp.int16(1), jnp.int16(0))
        accs = list(accs)
        for r in range(TRIP // PACK):
            accs[r % len(accs)] = accs[r % len(accs)] + ind[r]
        return tuple(accs)

    zero = jnp.zeros((PACK, LANES), I16)
    a = lax.fori_loop(0, n_trips, body, (zero, zero, zero, zero))
    tot = (a[0] + a[1]) + (a[2] + a[3])
    return jnp.sum(tot.astype(I32).astype(F32), axis=0, keepdims=True)


def _kth16(ref, n_trips, kf):
    c0 = _count16(ref, n_trips, jnp.zeros((1, LANES), I32), False)
    base = jnp.where(c0 >= kf, 0, I16_MIN).astype(I32)

    def bit_pass(p, base):
        cand = base | lax.shift_left(jnp.int32(1), 14 - p)
        c = _count16(ref, n_trips, cand, False)
        return jnp.where(c >= kf, cand, base)

    return lax.fori_loop(0, 15, bit_pass, base)


def _select_threshold(hi_ref, lo_ref, d2_ref, n_trips, k_sel):
    kf = jnp.full((1, LANES), float(k_sel), F32)
    hi_thr = _kth16(hi_ref, n_trips, kf)
    c_hi = _count16(hi_ref, n_trips, hi_thr, True)
    h16 = _row16(hi_thr)

    def fill(i, _):
        off = pl.multiple_of(i * TRIP, TRIP)
        d2 = jnp.where(_planes(hi_ref, off, TRIP) == h16, _planes(lo_ref, off, TRIP), jnp.int16(I16_MIN))
        d2_ref[pl.ds(off, TRIP), :] = d2.reshape(TRIP, LANES)
        return 0

    lax.fori_loop(0, n_trips, fill, 0)
    lo_thr = _kth16(d2_ref, n_trips, kf - c_hi)
    c_lo = _count16(d2_ref, n_trips, lo_thr, True)
    ties = jnp.where(hi_thr == I16_MIN, jnp.where(lo_thr == I16_MIN, 0.0, kf - c_hi - c_lo), kf - c_hi - c_lo)
    return hi_thr, lo_thr, ties


def _bias_pass(hi_ref, d2_ref, bias_ref, n_trips, hi_thr, lo_thr, ties):
    h16, l16 = _row16(hi_thr), _row16(lo_thr)
    r = lax.broadcasted_iota(I32, (SUB, SUB), 0)
    c = lax.broadcasted_iota(I32, (SUB, SUB), 1)
    tri = jnp.where(c <= r, 1.0, 0.0).astype(BF16)
    zero, one, neg = jnp.bfloat16(0.0), jnp.bfloat16(1.0), jnp.bfloat16(NEG_BIG)

    def body(i, carry):
        for sub in range(TRIP // SUB):
            off = pl.multiple_of(i * TRIP + sub * SUB, SUB)
            hi, d2 = _planes(hi_ref, off, SUB), _planes(d2_ref, off, SUB)
            tied = jnp.where(hi == h16, jnp.where(d2 == l16, one, zero), zero)
            rank = _dot(tri, tied.reshape(SUB, LANES)) + carry
            tie_bias = jnp.where(rank <= ties, 0.0, NEG_BIG).astype(BF16).reshape(SUB // PACK, PACK, LANES)
            bias = jnp.where(hi > h16, zero, jnp.where(d2 > l16, zero, jnp.where(tied > zero, tie_bias, neg)))
            bias_ref[pl.ds(off, SUB), :] = bias.reshape(SUB, LANES)
            carry = rank[SUB - 1:SUB, :]
        return carry

    lax.fori_loop(0, n_trips, body, jnp.zeros((1, LANES), F32))


def _attend(bias_ref, n_trips, k_blk, vt_blk, qs_ref, acc_ref, m_ref, reps):
    acc_ref[...] = jnp.zeros_like(acc_ref)
    m_ref[...] = jnp.full(m_ref.shape, NEG_BIG, F32)

    def body(i, _):
        off = pl.multiple_of(i * TRIP, TRIP)
        bias = bias_ref[pl.ds(off, TRIP), :].astype(F32)
        if reps > 1:
            bias = jnp.concatenate([bias] * reps, axis=1)
        for n in range(N_KV):
            s = _dot_nt(k_blk(n, off), qs_ref[n]) + bias
            m_old = m_ref[n]
            m_new = jnp.maximum(m_old, jnp.max(s, axis=0, keepdims=True))
            p = jnp.exp(s - m_new).astype(BF16)
            acc_ref[n] = acc_ref[n] * jnp.exp(m_old - m_new) + _dot(vt_blk(n, off), p)
            m_ref[n] = m_new
        return 0

    lax.fori_loop(0, n_trips, body, 0)


def _select_and_attend(hi_ref, lo_ref, d2_ref, bias_ref, n_trips, k_sel, k_blk, vt_blk, qs_ref, acc_ref, m_ref, reps):
    hi_thr, lo_thr, ties = _select_threshold(hi_ref, lo_ref, d2_ref, n_trips, k_sel)
    _bias_pass(hi_ref, d2_ref, bias_ref, n_trips, hi_thr, lo_thr, ties)
    _attend(bias_ref, n_trips, k_blk, vt_blk, qs_ref, acc_ref, m_ref, reps)


def _dsa_prompt_body(qi_ref, q_ref, g_ref, ki_ref, kg_ref, vt_ref, o_ref,
                     hi_ref, lo_ref, d2_ref, bias_ref, qs_ref, acc_ref, m_ref, *, tq, k_sel):
    j = pl.program_id(1)
    n_keys = (j + 1) * tq
    n_trips = (n_keys + TRIP - 1) // TRIP

    qis = qi_ref[...].reshape(IDX_HEADS * tq, IDX_DIM)
    qs_ref[0] = q_ref[0:GROUP].reshape(GROUP * tq, HEAD_DIM)
    qs_ref[1] = q_ref[GROUP:2 * GROUP].reshape(GROUP * tq, HEAD_DIM)
    gt = g_ref[...].T
    w_row = jnp.concatenate([gt[G_WI + h:G_WI + h + 1, :] for h in range(IDX_HEADS)], axis=1) * (IDX_HEADS ** -0.5)
    qpos = j * tq + lax.broadcasted_iota(I32, (1, LANES), 1)
    limit_row = (qpos // CHUNK + 1) * CHUNK

    def score(i, _):
        off = pl.multiple_of(i * TRIP, TRIP)
        hi, lo = _score_trip(ki_ref[pl.ds(off, TRIP), :], qis, w_row, limit_row, off, tq)
        hi_ref[pl.ds(off, TRIP), :] = hi
        lo_ref[pl.ds(off, TRIP), :] = lo
        return 0

    lax.fori_loop(0, n_trips, score, 0)
    _select_and_attend(hi_ref, lo_ref, d2_ref, bias_ref, n_trips, k_sel,
                       lambda n, off: kg_ref[n, pl.ds(off, TRIP), :],
                       lambda n, off: vt_ref[n, :, pl.ds(off, TRIP)],
                       qs_ref, acc_ref, m_ref, reps=GROUP)

    for n in range(N_KV):
        a = acc_ref[n]
        o_t = a[0:HEAD_DIM, :] / a[HEAD_DIM:HEAD_DIM + 1, :]
        for pair in range(GROUP // 2):
            two = jnp.concatenate([o_t[:, (2 * pair) * tq:(2 * pair + 1) * tq],
                                   o_t[:, (2 * pair + 1) * tq:(2 * pair + 2) * tq]], axis=0)
            col = (n * GROUP + 2 * pair) * HEAD_DIM
            o_ref[:, col:col + 2 * HEAD_DIM] = two.T


def _dsa_prompt(p, batch, seq, k_sel, *, tq=LANES):
    nq = seq // tq
    return pl.pallas_call(
        functools.partial(_dsa_prompt_body, tq=tq, k_sel=k_sel),
        grid=(batch, nq),
        in_specs=[
            pl.BlockSpec((IDX_HEADS, tq, IDX_DIM), lambda b, j: (0, b * nq + j, 0)),
            pl.BlockSpec((N_HEADS, tq, HEAD_DIM), lambda b, j: (0, b * nq + j, 0)),
            pl.BlockSpec((tq, LANES), lambda b, j: (b * nq + j, 0)),
            pl.BlockSpec((seq, IDX_DIM), lambda b, j: (b, 0)),
            pl.BlockSpec((N_KV, seq, HEAD_DIM), lambda b, j: (0, b, 0)),
            pl.BlockSpec((N_KV, 2 * HEAD_DIM, seq), lambda b, j: (0, 0, b)),
        ],
        out_specs=pl.BlockSpec((tq, N_HEADS * HEAD_DIM), lambda b, j: (b * nq + j, 0)),
        out_shape=jax.ShapeDtypeStruct((batch * seq, N_HEADS * HEAD_DIM), F32),
        scratch_shapes=[
            pltpu.VMEM((seq, LANES), I16), pltpu.VMEM((seq, LANES), I16), pltpu.VMEM((seq, LANES), I16),
            pltpu.VMEM((seq, LANES), BF16),
            pltpu.VMEM((N_KV, GROUP * tq, HEAD_DIM), BF16),
            pltpu.VMEM((N_KV, 2 * HEAD_DIM, GROUP * tq), F32),
            pltpu.VMEM((N_KV, 1, GROUP * tq), F32),
        ],
        compiler_params=_params(("arbitrary", "arbitrary")),
        name="dsa_prompt",
    )(p["qi"], p["q"], p["gt"], p["kib"], p["kg"], p["vt"])


def _dsa_sample_body(qi_ref, q_ref, g_ref, kin_ref, kn_ref, vn_ref, kc_ref, vc_ref, ic_ref, o_ref,
                     hi_ref, lo_ref, d2_ref, bias_ref, kib_scr, kg_scr, vt_scr, qs_ref, acc_ref, m_ref,
                     *, tq, past, k_sel):
    l_pad = hi_ref.shape[0]
    n_trips = l_pad // TRIP
    n_keys = past + tq

    fill = 512
    for c in range(past // fill):
        rows = slice(c * fill, (c + 1) * fill)
        kc = kc_ref[rows, :]
        vt = vc_ref[rows, :].T
        kib_scr[rows, :] = ic_ref[rows, :].astype(BF16)
        for n in range(N_KV):
            kg_scr[n, rows, :] = kc[:, n * HEAD_DIM:(n + 1) * HEAD_DIM].astype(BF16)
            vt_scr[n, 0:HEAD_DIM, rows] = vt[n * HEAD_DIM:(n + 1) * HEAD_DIM, :].astype(BF16)
    tail = l_pad - past
    kn = jnp.concatenate([kn_ref[...], jnp.zeros((tail - tq, N_KV * HEAD_DIM), F32)], axis=0)
    vn = jnp.concatenate([vn_ref[...], jnp.zeros((tail - tq, N_KV * HEAD_DIM), F32)], axis=0).T
    kib_scr[past:l_pad, :] = jnp.concatenate(
        [kin_ref[...], jnp.zeros((tail - tq, IDX_DIM), F32)], axis=0).astype(BF16)
    for n in range(N_KV):
        kg_scr[n, past:l_pad, :] = kn[:, n * HEAD_DIM:(n + 1) * HEAD_DIM].astype(BF16)
        vt_scr[n, 0:HEAD_DIM, past:l_pad] = vn[n * HEAD_DIM:(n + 1) * HEAD_DIM, :].astype(BF16)
        vt_scr[n, HEAD_DIM:2 * HEAD_DIM, :] = jnp.ones((HEAD_DIM, l_pad), BF16)

    qis = qi_ref[...].reshape(IDX_HEADS * tq, IDX_DIM)
    qs_ref[0] = q_ref[0:GROUP].reshape(GROUP * tq, HEAD_DIM)
    qs_ref[1] = q_ref[GROUP:2 * GROUP].reshape(GROUP * tq, HEAD_DIM)
    eye = jnp.where(lax.broadcasted_iota(I32, (LANES, LANES), 0) == lax.broadcasted_iota(I32, (LANES, LANES), 1),
                    1.0, 0.0)
    gt = lax.dot_general(eye, g_ref[...], (((1,), (1,)), ((), ())),
                         precision=lax.Precision.HIGHEST, preferred_element_type=F32)
    w_row = jnp.concatenate([gt[G_WI + h:G_WI + h + 1, :] for h in range(IDX_HEADS)], axis=1) * (IDX_HEADS ** -0.5)
    limit_row = jnp.full((1, LANES), n_keys, I32)

    for i in range(n_trips):
        off = i * TRIP
        hi, lo = _score_trip(kib_scr[off:off + TRIP, :], qis, w_row, limit_row, off, tq)
        hi_ref[off:off + TRIP, :] = hi
        lo_ref[off:off + TRIP, :] = lo
    _select_and_attend(hi_ref, lo_ref, d2_ref, bias_ref, n_trips, k_sel,
                       lambda n, off: kg_scr[n, pl.ds(off, TRIP), :],
                       lambda n, off: vt_scr[n, :, pl.ds(off, TRIP)],
                       qs_ref, acc_ref, m_ref, reps=1)

    for n in range(N_KV):
        a = acc_ref[n]
        o = (a[0:HEAD_DIM, :] / a[HEAD_DIM:HEAD_DIM + 1, :]).T
        for pair in range(GROUP // 2):
            col = (n * GROUP + 2 * pair) * HEAD_DIM
            o_ref[:, col:col + 2 * HEAD_DIM] = jnp.concatenate(
                [o[(2 * pair) * tq:(2 * pair + 1) * tq, :], o[(2 * pair + 1) * tq:(2 * pair + 2) * tq, :]], axis=1)


def _dsa_sample(p, row0, streams, tq, cache_k, cache_v, cache_ik, k_sel):
    past = cache_k.shape[1]
    l_pad = ((past + tq + TRIP - 1) // TRIP) * TRIP
    b0 = row0 // tq
    return pl.pallas_call(
        functools.partial(_dsa_sample_body, tq=tq, past=past, k_sel=k_sel),
        grid=(streams,),
        in_specs=[
            pl.BlockSpec((IDX_HEADS, tq, IDX_DIM), lambda s: (0, b0 + s, 0)),
            pl.BlockSpec((N_HEADS, tq, HEAD_DIM), lambda s: (0, b0 + s, 0)),
            pl.BlockSpec((tq, LANES), lambda s: (b0 + s, 0)),
            pl.BlockSpec((tq, IDX_DIM), lambda s: (b0 + s, 0)),
            pl.BlockSpec((tq, N_KV * HEAD_DIM), lambda s: (b0 + s, 0)),
            pl.BlockSpec((tq, N_KV * HEAD_DIM), lambda s: (b0 + s, 0)),
            pl.BlockSpec((None, past, N_KV * HEAD_DIM), lambda s: (s, 0, 0)),
            pl.BlockSpec((None, past, N_KV * HEAD_DIM), lambda s: (s, 0, 0)),
            pl.BlockSpec((None, past, IDX_DIM), lambda s: (s, 0, 0)),
        ],
        out_specs=pl.BlockSpec((tq, N_HEADS * HEAD_DIM), lambda s: (s, 0)),
        out_shape=jax.ShapeDtypeStruct((streams * tq, N_HEADS * HEAD_DIM), F32),
        scratch_shapes=[
            pltpu.VMEM((l_pad, LANES), I16), pltpu.VMEM((l_pad, LANES), I16), pltpu.VMEM((l_pad, LANES), I16),
            pltpu.VMEM((l_pad, LANES), BF16),
            pltpu.VMEM((l_pad, IDX_DIM), BF16),
            pltpu.VMEM((N_KV, l_pad, HEAD_DIM), BF16),
            pltpu.VMEM((N_KV, 2 * HEAD_DIM, l_pad), BF16),
            pltpu.VMEM((N_KV, GROUP * tq, HEAD_DIM), BF16),
            pltpu.VMEM((N_KV, 2 * HEAD_DIM, GROUP * tq), F32),
            pltpu.VMEM((N_KV, 1, GROUP * tq), F32),
        ],
        compiler_params=_params(("arbitrary",)),
        name="dsa_sample",
    )(p["qi"], p["q"], p["gt"], p["kif"], p["kf"], p["vf"], cache_k, cache_v, cache_ik)


def _log_sigmoid(x):
    return jnp.minimum(x, 0.0) - jnp.log1p(jnp.exp(-jnp.abs(x)))


def _mlstm_body(mq_ref, mk_ref, mv_ref, g_ref, bias_ref, c0_ref, n0_ref, m0_ref,
                h_ref, c_ref, n_ref, m_ref, *, t):
    @pl.when(pl.program_id(1) == 0)
    def _():
        c_ref[...] = c0_ref[...]
        n_ref[...] = n0_ref[...]
        m_ref[...] = m0_ref[...]

    gb = g_ref[...] + bias_ref[0:1, :]
    eye = jnp.where(lax.broadcasted_iota(I32, (LANES, LANES), 0) == lax.broadcasted_iota(I32, (LANES, LANES), 1),
                    1.0, 0.0)
    gbt = lax.dot_general(eye, gb, (((1,), (1,)), ((), ())),
                          precision=lax.Precision.HIGHEST, preferred_element_type=F32)
    r = lax.broadcasted_iota(I32, (t, t), 0)
    c = lax.broadcasted_iota(I32, (t, t), 1)
    causal = c <= r
    for h in range(M_HEADS):
        ig_row = gbt[G_MI + h:G_MI + h + 1, :]
        lf_row = _log_sigmoid(gbt[G_MF + h:G_MF + h + 1, :])
        ig_col = gb[:, G_MI + h:G_MI + h + 1]
        lf_col = _log_sigmoid(gb[:, G_MF + h:G_MF + h + 1])
        b_col = jnp.sum(jnp.where(causal, lf_row, 0.0), axis=1, keepdims=True)
        b_row = jnp.sum(jnp.where(r <= c, lf_col, 0.0), axis=0, keepdims=True)
        b_last = b_col[t - 1:t, :]
        m_prev = m_ref[0, h:h + 1, 0:1]
        d = jnp.where(causal, b_col - b_row + ig_row, -jnp.inf)
        m_inter = b_col + m_prev
        m_t = jnp.maximum(m_inter, jnp.max(d, axis=1, keepdims=True))
        sl = slice(h * M_DIM, (h + 1) * M_DIM)
        q = mq_ref[:, sl]
        k = mk_ref[:, sl] * (M_DIM ** -0.5)
        v = mv_ref[:, sl]
        qb, kb = q.astype(BF16), k.astype(BF16)
        sc = _dot_nt(qb, kb) * jnp.exp(d - m_t)
        decay = jnp.exp(m_inter - m_t)
        c_old = c_ref[0, h]
        n_old = n_ref[0, h:h + 1, :]
        num = _dot(sc.astype(BF16), v.astype(BF16)) + decay * _dot_nt(qb, c_old.astype(BF16))
        den = jnp.sum(sc, axis=1, keepdims=True) + decay * jnp.sum(q * n_old, axis=1, keepdims=True)
        h_ref[:, sl] = num / jnp.maximum(jnp.abs(den), jnp.exp(-m_t))
        m_new = m_t[t - 1:t, :]
        g_col = jnp.exp(b_last - b_col + ig_col - m_new)
        dec_end = jnp.exp(b_last + m_prev - m_new)
        vg_t = (v * g_col).T.astype(BF16)
        c_ref[0, h] = dec_end * c_old + _dot(vg_t, kb)
        n_ref[0, h:h + 1, :] = dec_end * n_old + jnp.sum(g_col * k, axis=0, keepdims=True)
        m_ref[0, h:h + 1, :] = jnp.broadcast_to(m_new, (1, LANES))


def _mlstm(p, row0, batch, seq, t, bias, c0, n0, m0):
    nc = seq // t
    b0 = row0 // t
    hm = M_HEADS * M_DIM
    tok = lambda b, c: (b0 + b * nc + c, 0)
    st4 = lambda b, c: (b, 0, 0, 0)
    st3 = lambda b, c: (b, 0, 0)
    return pl.pallas_call(
        functools.partial(_mlstm_body, t=t),
        grid=(batch, nc),
        in_specs=[
            pl.BlockSpec((t, hm), tok), pl.BlockSpec((t, hm), tok), pl.BlockSpec((t, hm), tok),
            pl.BlockSpec((t, LANES), tok),
            pl.BlockSpec((8, LANES), lambda b, c: (0, 0)),
            pl.BlockSpec((1, M_HEADS, M_DIM, M_DIM), st4),
            pl.BlockSpec((1, M_HEADS, M_DIM), st3),
            pl.BlockSpec((1, M_HEADS, LANES), st3),
        ],
        out_specs=[
            pl.BlockSpec((t, hm), lambda b, c: (b * nc + c, 0)),
            pl.BlockSpec((1, M_HEADS, M_DIM, M_DIM), st4),
            pl.BlockSpec((1, M_HEADS, M_DIM), st3),
            pl.BlockSpec((1, M_HEADS, LANES), st3),
        ],
        out_shape=[
            jax.ShapeDtypeStruct((batch * seq, hm), F32),
            jax.ShapeDtypeStruct((batch, M_HEADS, M_DIM, M_DIM), F32),
            jax.ShapeDtypeStruct((batch, M_HEADS, M_DIM), F32),
            jax.ShapeDtypeStruct((batch, M_HEADS, LANES), F32),
        ],
        compiler_params=_params(("arbitrary", "arbitrary")),
        name="mlstm",
    )(p["mq"], p["mk"], p["mv"], p["gt"], bias, c0, n0, m0)


def _merge_body(x_ref, attn_ref, mh_ref, mo_ref, ga_ref, gm_ref, gml_ref, wap_ref, wmp_ref, wo_ref, o_ref):
    a = _dot(attn_ref[...].astype(BF16), wap_ref[...])
    parts = []
    for h in range(M_HEADS):
        sl = slice(h * M_DIM, (h + 1) * M_DIM)
        parts.append(_rms(mh_ref[:, sl], gml_ref[:, sl]))
    hn = jnp.concatenate(parts, axis=1)
    b = _dot((jax.nn.sigmoid(mo_ref[...]) * hn).astype(BF16), wmp_ref[...])
    mix = jax.nn.sigmoid(ga_ref[...]) * a + jax.nn.sigmoid(gm_ref[...]) * b
    o_ref[...] = x_ref[...] + _dot(mix.astype(BF16), wo_ref[...])


def _merge(x, attn, mh, p, g_ml, w_ap, w_mp, w_o, *, tm=256):
    n, d = x.shape
    tm = _tile(n, tm)
    ha, hm = N_HEADS * HEAD_DIM, M_HEADS * M_DIM
    row = lambda i: (i, 0)
    fix = lambda i: (0, 0)
    return pl.pallas_call(
        _merge_body,
        grid=(n // tm,),
        in_specs=[
            pl.BlockSpec((tm, d), row), pl.BlockSpec((tm, ha), row), pl.BlockSpec((tm, hm), row),
            pl.BlockSpec((tm, hm), row), pl.BlockSpec((tm, d), row), pl.BlockSpec((tm, d), row),
            pl.BlockSpec((1, hm), fix), pl.BlockSpec((ha, d), fix), pl.BlockSpec((hm, d), fix),
            pl.BlockSpec((d, d), fix),
        ],
        out_specs=pl.BlockSpec((tm, d), row),
        out_shape=jax.ShapeDtypeStruct((n, d), F32),
        compiler_params=_params(("parallel",)),
        name="merge",
    )(x, attn, mh, p["mo"], p["ga"], p["gm"], g_ml.reshape(1, hm), w_ap, w_mp, w_o)


def _rearrange_w_in(w):
    cuts = [int(c) for c in np.cumsum(SPLIT_SIZES)[:-1]]
    q, k, v, qi, ki, wi, mq, mk, mv, mi, mf, mo, ga, gm = jnp.split(w, cuts, axis=1)
    pad = jnp.zeros((w.shape[0], LANES - IDX_DIM - 3 * IDX_HEADS), w.dtype)
    return jnp.concatenate([q, k, v, qi, ki, wi, mi, mf, pad, mq, mk, mv, mo, ga, gm], axis=1).astype(BF16)


def _rope_tables(pos):
    freqs = ROPE_THETA ** (-jnp.arange(ROT_HALF, dtype=F32) / ROT_HALF)
    ang = pos.astype(F32)[:, None] * freqs[None, :]
    cos, sin = jnp.cos(ang), jnp.sin(ang)
    n = pos.shape[0]
    one = jnp.ones((n, HEAD_DIM - 2 * ROT_HALF), F32)
    zero8 = jnp.zeros((n, ROT_HALF), F32)
    zero = jnp.zeros((n, HEAD_DIM - 2 * ROT_HALF), F32)
    two = lambda a: jnp.concatenate([a, a], axis=1)
    return (two(jnp.concatenate([cos, cos, one], axis=1)),
            two(jnp.concatenate([zero8, sin, zero], axis=1)),
            two(jnp.concatenate([-sin, zero8, zero], axis=1)))


def kernel(x_prompt, x_sample, cache_k, cache_v, cache_idx_k, state_C, state_n, state_m,
           g_ffn1, w_ffn1_gate, w_ffn1_up, w_ffn1_down, g_mix, w_in, b_igate, b_fgate,
           g_mlstm, w_attn_proj, w_mlstm_proj, w_out, g_ffn2, w_ffn2_gate, w_ffn2_up,
           w_ffn2_down, g_final):
    batch, seq, d = x_prompt.shape
    streams, t_s, _ = x_sample.shape
    depth = w_in.shape[0]
    past = cache_k.shape[2]
    n_p, n_s = batch * seq, streams * t_s
    proj_tm = 256
    t_prompt = 256

    x = jnp.concatenate([x_prompt.reshape(n_p, d), x_sample.reshape(n_s, d)], axis=0)

    pos = jnp.concatenate([jnp.arange(seq, dtype=I32),
                           past + (jnp.arange(proj_tm, dtype=I32) % t_s)])
    cos_t, s1_t, s2_t = _rope_tables(pos)
    tiles_per_seq = seq // proj_tm
    n_prompt_tiles = n_p // proj_tm
    tab_block = lambda i: jnp.where(i < n_prompt_tiles, i % tiles_per_seq, tiles_per_seq)

    k_sel_p = min(TOPK_MAX, seq // 4)
    k_sel_s = min(TOPK_MAX, (past + t_s) // 4)

    kp, vp, ikp, cp, np_, mp = [], [], [], [], [], []
    ks, vs, iks, cs, ns, ms = [], [], [], [], [], []
    for l in range(depth):
        last = l == depth - 1
        x = _ffn(x, g_ffn1[l], w_ffn1_gate[l].astype(BF16), w_ffn1_up[l].astype(BF16),
                 w_ffn1_down[l].astype(BF16), g_final, final_norm=False)
        p = _proj(x, g_mix[l], _rearrange_w_in(w_in[l]), cos_t, s1_t, s2_t, tab_block, tm=proj_tm)

        attn_p = _dsa_prompt(p, batch, seq, k_sel_p)
        attn_s = _dsa_sample(p, n_p, streams, t_s,
                             cache_k[l].reshape(streams, past, N_KV * HEAD_DIM),
                             cache_v[l].reshape(streams, past, N_KV * HEAD_DIM),
                             cache_idx_k[l], k_sel_s)

        bias = jnp.zeros((8, LANES), F32)
        bias = bias.at[0, G_MI:G_MI + M_HEADS].set(b_igate[l]).at[0, G_MF:G_MF + M_HEADS].set(b_fgate[l])
        mh_p, c_p, n_pr, m_p = _mlstm(
            p, 0, batch, seq, t_prompt, bias,
            jnp.zeros((batch, M_HEADS, M_DIM, M_DIM), F32), jnp.zeros((batch, M_HEADS, M_DIM), F32),
            jnp.full((batch, M_HEADS, LANES), M_NEG, F32))
        mh_s, c_s, n_sm, m_s = _mlstm(
            p, n_p, streams, t_s, t_s, bias, state_C[l], state_n[l],
            jnp.broadcast_to(state_m[l][:, :, None], (streams, M_HEADS, LANES)))

        x = _merge(x, jnp.concatenate([attn_p, attn_s], axis=0), jnp.concatenate([mh_p, mh_s], axis=0), p,
                   g_mlstm[l], w_attn_proj[l].astype(BF16), w_mlstm_proj[l].astype(BF16), w_out[l].astype(BF16))
        x = _ffn(x, g_ffn2[l], w_ffn2_gate[l].astype(BF16), w_ffn2_up[l].astype(BF16),
                 w_ffn2_down[l].astype(BF16), g_final, final_norm=last)

        kp.append(p["kf"][:n_p].reshape(batch, seq, N_KV, HEAD_DIM))
        vp.append(p["vf"][:n_p].reshape(batch, seq, N_KV, HEAD_DIM))
        ikp.append(p["kif"][:n_p].reshape(batch, seq, IDX_DIM))
        cp.append(c_p); np_.append(n_pr); mp.append(m_p[:, :, 0])
        ks.append(p["kf"][n_p:].reshape(streams, t_s, N_KV, HEAD_DIM))
        vs.append(p["vf"][n_p:].reshape(streams, t_s, N_KV, HEAD_DIM))
        iks.append(p["kif"][n_p:].reshape(streams, t_s, IDX_DIM))
        cs.append(c_s); ns.append(n_sm); ms.append(m_s[:, :, 0])

    st = jnp.stack
    return (x[:n_p].reshape(batch, seq, d), x[n_p:].reshape(streams, t_s, d),
            st(kp), st(vp), st(ikp), st(cp), st(np_), st(mp),
            st(ks), st(vs), st(iks), st(cs), st(ns), st(ms))
```

```python
import functools

import jax
import jax.numpy as jnp
import numpy as np
from jax import lax
from jax.experimental import pallas as pl
from jax.experimental.pallas import tpu as pltpu

F32 = jnp.float32
BF16 = jnp.bfloat16
I32 = jnp.int32

D_MODEL = 1024
D_FF = 2816
N_HEADS = 8
N_KV = 2
GROUP = N_HEADS // N_KV
HEAD_DIM = 64
IDX_HEADS = 4
IDX_DIM = 64
ROT_HALF = 8
ROPE_THETA = 500000.0
M_HEADS = 4
M_DIM = 128
CHUNK = 64
TOPK_MAX = 256
EPS = 1e-6
M_NEG = -1e30
PAST_LEN = 4096
SPLIT_SIZES = (N_HEADS * HEAD_DIM, N_KV * HEAD_DIM, N_KV * HEAD_DIM,
               IDX_HEADS * IDX_DIM, IDX_DIM, IDX_HEADS,
               M_HEADS * M_DIM, M_HEADS * M_DIM, M_HEADS * M_DIM, M_HEADS, M_HEADS, M_HEADS * M_DIM,
               D_MODEL, D_MODEL)

LANES = 128
INT_MIN = -2 ** 31
NEG_BIG = -1e30
VMEM_LIMIT = 48 * 1024 * 1024

OFF_Q, OFF_K, OFF_V, OFF_QI, OFF_G = 0, 512, 640, 768, 1024
OFF_MQ, OFF_MK, OFF_MV, OFF_MO, OFF_GA, OFF_GM = 1152, 1664, 2176, 2688, 3200, 4224
PROJ_COLS = 5248
G_WI, G_MI, G_MF = 64, 68, 72


def _params(sem):
    return pltpu.CompilerParams(dimension_semantics=sem, vmem_limit_bytes=VMEM_LIMIT)


def _dot(a, b):
    return jnp.dot(a, b, preferred_element_type=F32)


def _dot_nt(a, b):
    return lax.dot_general(a, b, (((1,), (1,)), ((), ())), preferred_element_type=F32)


def _tile(n, pref):
    while n % pref:
        pref //= 2
    return pref


def _rms(x, g):
    return x * lax.rsqrt(jnp.mean(x * x, axis=-1, keepdims=True) + EPS) * g


def _ffn_body(x_ref, g_ref, wg_ref, wu_ref, wd_ref, gf_ref, o_ref, h_scr, acc_scr, *, final_norm):
    f = pl.program_id(1)

    @pl.when(f == 0)
    def _():
        h_scr[...] = _rms(x_ref[...], g_ref[...]).astype(BF16)
        acc_scr[...] = jnp.zeros_like(acc_scr)

    h = h_scr[...]
    gate = _dot(h, wg_ref[...])
    up = _dot(h, wu_ref[...])
    act = (gate * jax.nn.sigmoid(gate) * up).astype(BF16)
    acc_scr[...] += _dot(act, wd_ref[...])

    @pl.when(f == pl.num_programs(1) - 1)
    def _():
        y = x_ref[...] + 0.5 * acc_scr[...]
        if final_norm:
            y = _rms(y, gf_ref[...])
        o_ref[...] = y


def _ffn(x, g, wg, wu, wd, g_final, *, final_norm, tm=512, tf=1408):
    n, d = x.shape
    dff = wg.shape[1]
    tm = _tile(n, tm)
    assert dff % tf == 0
    return pl.pallas_call(
        functools.partial(_ffn_body, final_norm=final_norm),
        grid=(n // tm, dff // tf),
        in_specs=[
            pl.BlockSpec((tm, d), lambda i, f: (i, 0)),
            pl.BlockSpec((1, d), lambda i, f: (0, 0)),
            pl.BlockSpec((d, tf), lambda i, f: (0, f)),
            pl.BlockSpec((d, tf), lambda i, f: (0, f)),
            pl.BlockSpec((tf, d), lambda i, f: (f, 0)),
            pl.BlockSpec((1, d), lambda i, f: (0, 0)),
        ],
        out_specs=pl.BlockSpec((tm, d), lambda i, f: (i, 0)),
        out_shape=jax.ShapeDtypeStruct((n, d), F32),
        scratch_shapes=[pltpu.VMEM((tm, d), BF16), pltpu.VMEM((tm, d), F32)],
        compiler_params=_params(("parallel", "arbitrary")),
        name="ffn",
    )(x, g.reshape(1, d), wg, wu, wd, g_final.reshape(1, d))


def _rope_tile(x, c, s1, s2):
    return x * c + pltpu.roll(x, ROT_HALF, 1) * s1 + pltpu.roll(x, LANES - ROT_HALF, 1) * s2


def _proj_body(x_ref, g_ref, w_ref, c_ref, s1_ref, s2_ref,
               q_ref, kg_ref, kf_ref, vt_ref, vf_ref, qi_ref, kib_ref, kif_ref, gt_ref,
               mq_ref, mk_ref, mv_ref, mo_ref, ga_ref, gm_ref):
    h = _rms(x_ref[...], g_ref[...]).astype(BF16)
    c, s1, s2 = c_ref[...], s1_ref[...], s2_ref[...]

    def seg(off, width):
        return _dot(h, w_ref[:, off:off + width])

    def rope(z):
        tiles = [_rope_tile(z[:, t * LANES:(t + 1) * LANES], c, s1, s2) for t in range(z.shape[1] // LANES)]
        return tiles[0] if len(tiles) == 1 else jnp.concatenate(tiles, axis=1)

    q = rope(seg(OFF_Q, N_HEADS * HEAD_DIM)) * (HEAD_DIM ** -0.5)
    for hd in range(N_HEADS):
        q_ref[hd] = q[:, hd * HEAD_DIM:(hd + 1) * HEAD_DIM].astype(BF16)

    k = rope(seg(OFF_K, N_KV * HEAD_DIM))
    kf_ref[...] = k
    for n in range(N_KV):
        kg_ref[n] = k[:, n * HEAD_DIM:(n + 1) * HEAD_DIM].astype(BF16)

    v = seg(OFF_V, N_KV * HEAD_DIM)
    vf_ref[...] = v
    vt = v.T
    ones = jnp.ones((HEAD_DIM, vt.shape[1]), BF16)
    for n in range(N_KV):
        vt_ref[n, 0:HEAD_DIM, :] = vt[n * HEAD_DIM:(n + 1) * HEAD_DIM, :].astype(BF16)
        vt_ref[n, HEAD_DIM:2 * HEAD_DIM, :] = ones

    qi = rope(seg(OFF_QI, IDX_HEADS * IDX_DIM)) * (IDX_DIM ** -0.5)
    for hd in range(IDX_HEADS):
        qi_ref[hd] = qi[:, hd * IDX_DIM:(hd + 1) * IDX_DIM].astype(BF16)

    gz = seg(OFF_G, LANES)
    lane = lax.broadcasted_iota(I32, gz.shape, 1)
    is_ki = lane < IDX_DIM
    gz = _rope_tile(gz, jnp.where(is_ki, c, 1.0), jnp.where(is_ki, s1, 0.0), jnp.where(is_ki, s2, 0.0))
    gt_ref[...] = gz
    kif_ref[...] = gz[:, 0:IDX_DIM]
    kib_ref[...] = gz[:, 0:IDX_DIM].astype(BF16)

    mq_ref[...] = seg(OFF_MQ, M_HEADS * M_DIM)
    mk_ref[...] = seg(OFF_MK, M_HEADS * M_DIM)
    mv_ref[...] = seg(OFF_MV, M_HEADS * M_DIM)
    mo_ref[...] = seg(OFF_MO, M_HEADS * M_DIM)
    ga_ref[...] = seg(OFF_GA, D_MODEL)
    gm_ref[...] = seg(OFF_GM, D_MODEL)


def _proj(x, g, w, cos_t, s1_t, s2_t, tab_block, *, tm=256):
    n, d = x.shape
    assert n % tm == 0
    hm = M_HEADS * M_DIM
    row = lambda i: (i, 0)
    tab = lambda i: (tab_block(i), 0)
    out_shapes = dict(
        q=((N_HEADS, n, HEAD_DIM), BF16, pl.BlockSpec((N_HEADS, tm, HEAD_DIM), lambda i: (0, i, 0))),
        kg=((N_KV, n, HEAD_DIM), BF16, pl.BlockSpec((N_KV, tm, HEAD_DIM), lambda i: (0, i, 0))),
        kf=((n, N_KV * HEAD_DIM), F32, pl.BlockSpec((tm, N_KV * HEAD_DIM), row)),
        vt=((N_KV, 2 * HEAD_DIM, n), BF16, pl.BlockSpec((N_KV, 2 * HEAD_DIM, tm), lambda i: (0, 0, i))),
        vf=((n, N_KV * HEAD_DIM), F32, pl.BlockSpec((tm, N_KV * HEAD_DIM), row)),
        qi=((IDX_HEADS, n, IDX_DIM), BF16, pl.BlockSpec((IDX_HEADS, tm, IDX_DIM), lambda i: (0, i, 0))),
        kib=((n, IDX_DIM), BF16, pl.BlockSpec((tm, IDX_DIM), row)),
        kif=((n, IDX_DIM), F32, pl.BlockSpec((tm, IDX_DIM), row)),
        gt=((n, LANES), F32, pl.BlockSpec((tm, LANES), row)),
        mq=((n, hm), F32, pl.BlockSpec((tm, hm), row)),
        mk=((n, hm), F32, pl.BlockSpec((tm, hm), row)),
        mv=((n, hm), F32, pl.BlockSpec((tm, hm), row)),
        mo=((n, hm), F32, pl.BlockSpec((tm, hm), row)),
        ga=((n, D_MODEL), F32, pl.BlockSpec((tm, D_MODEL), row)),
        gm=((n, D_MODEL), F32, pl.BlockSpec((tm, D_MODEL), row)),
    )
    names = list(out_shapes)
    outs = pl.pallas_call(
        _proj_body,
        grid=(n // tm,),
        in_specs=[
            pl.BlockSpec((tm, d), row),
            pl.BlockSpec((1, d), lambda i: (0, 0)),
            pl.BlockSpec((d, PROJ_COLS), lambda i: (0, 0)),
            pl.BlockSpec((tm, LANES), tab),
            pl.BlockSpec((tm, LANES), tab),
            pl.BlockSpec((tm, LANES), tab),
        ],
        out_specs=[out_shapes[k][2] for k in names],
        out_shape=[jax.ShapeDtypeStruct(out_shapes[k][0], out_shapes[k][1]) for k in names],
        compiler_params=_params(("parallel",)),
        name="proj",
    )(x, g.reshape(1, d), w, cos_t, s1_t, s2_t)
    return dict(zip(names, outs))


TRIP = 512
SUBL = 8
REP = (SUBL, LANES)


def _sortable(score):
    score = jnp.where(score == 0.0, 0.0, score)
    bits = lax.bitcast_convert_type(score, I32)
    return bits ^ ((bits >> 31) & 0x7FFFFFFF)


def _head_sum(x, tq):
    if tq == LANES:
        return x[:, 0:128] + x[:, 128:256] + x[:, 256:384] + x[:, 384:512]
    assert 4 * tq == LANES
    return x + pltpu.roll(x, tq, 1) + pltpu.roll(x, 2 * tq, 1) + pltpu.roll(x, 3 * tq, 1)


def _score_trip(ki_blk, qis, w_row, limit_row, off, tq):
    lg = jnp.maximum(_dot_nt(ki_blk, qis), 0.0) * w_row
    key = _sortable(_head_sum(lg, tq))
    pos = off + lax.broadcasted_iota(I32, key.shape, 0)
    return jnp.where(pos < limit_row, key, INT_MIN)


def _all_sublanes(x):
    x = x + pltpu.roll(x, 4, 0)
    x = x + pltpu.roll(x, 2, 0)
    return x + pltpu.roll(x, 1, 0)


def _planes(ref, off, rows):
    return ref[pl.ds(off, rows), :].reshape(rows // SUBL, SUBL, LANES)


def _count_ge(ref, n_trips, cand):
    def body(i, accs):
        t = _planes(ref, pl.multiple_of(i * TRIP, TRIP), TRIP)
        ind = jnp.where(t >= cand, 1, 0)
        accs = list(accs)
        for r in range(TRIP // SUBL):
            accs[r % len(accs)] = accs[r % len(accs)] + ind[r]
        return tuple(accs)

    zero = jnp.zeros(REP, I32)
    a = lax.fori_loop(0, n_trips, body, (zero, zero, zero, zero))
    return _all_sublanes((a[0] + a[1]) + (a[2] + a[3])).astype(F32)


def _kth_largest(ref, n_trips, kf, total):
    c0 = _count_ge(ref, n_trips, jnp.zeros(REP, I32))
    ok = c0 >= kf
    state = (jnp.where(ok, 0, INT_MIN).astype(I32), jnp.where(ok, c0, total), jnp.where(ok, 0.0, c0))

    def bit_pass(p, state):
        base, c_ge, c_gt = state
        cand = base | lax.shift_left(jnp.int32(1), 30 - p)
        c = _count_ge(ref, n_trips, cand)
        ok = c >= kf
        return jnp.where(ok, cand, base), jnp.where(ok, c, c_ge), jnp.where(ok, c_gt, c)

    return lax.fori_loop(0, 31, bit_pass, state)


def _kth_pos(ref, n_trips, kf, nbits):
    def bit_pass(p, base):
        cand = base | lax.shift_left(jnp.int32(1), nbits - 1 - p)
        return jnp.where(_count_ge(ref, n_trips, cand) >= kf, cand, base)

    return lax.fori_loop(0, nbits, bit_pass, jnp.zeros(REP, I32))


def _select(key_ref, pos_ref, bias_ref, n_trips, k_sel):
    l_pad = key_ref.shape[0]
    total = jnp.full(REP, 1.0, F32) * (n_trips * TRIP)
    kf = jnp.full(REP, float(k_sel), F32)
    thr, c_ge, c_gt = _kth_largest(key_ref, n_trips, kf, total)
    ties = kf - c_gt
    shape3 = (TRIP // SUBL, SUBL, LANES)
    row_in_trip = lax.broadcasted_iota(I32, shape3, 0) * SUBL + lax.broadcasted_iota(I32, shape3, 1)

    def fill_pos(i, _):
        off = pl.multiple_of(i * TRIP, TRIP)
        rev = (l_pad - 1 - off) - row_in_trip
        pos_ref[pl.ds(off, TRIP), :] = jnp.where(_planes(key_ref, off, TRIP) == thr, rev, -1).reshape(TRIP, LANES)
        return 0

    lax.fori_loop(0, n_trips, fill_pos, 0)
    short = thr == INT_MIN
    crowded = jnp.where(c_ge - c_gt > ties, jnp.where(short, 0.0, 1.0), 0.0)
    nbits = max(1, (l_pad - 1).bit_length())
    pos_thr = lax.cond(jnp.max(crowded) > 0.0,
                       lambda: _kth_pos(pos_ref, n_trips, ties, nbits),
                       lambda: jnp.zeros(REP, I32))
    pos_thr = jnp.where(short, l_pad, pos_thr)

    def fill_bias(i, _):
        off = pl.multiple_of(i * TRIP, TRIP)
        bias = jnp.where(_planes(key_ref, off, TRIP) > thr, 0.0,
                         jnp.where(_planes(pos_ref, off, TRIP) >= pos_thr, 0.0, NEG_BIG))
        bias_ref[pl.ds(off, TRIP), :] = bias.reshape(TRIP, LANES).astype(BF16)
        return 0

    lax.fori_loop(0, n_trips, fill_bias, 0)


def _attend(bias_ref, n_trips, k_blk, vt_blk, qs_ref, acc_ref, m_ref, reps):
    acc_ref[...] = jnp.zeros_like(acc_ref)
    m_ref[...] = jnp.full(m_ref.shape, NEG_BIG, F32)

    def body(i, _):
        off = pl.multiple_of(i * TRIP, TRIP)
        bias = bias_ref[pl.ds(off, TRIP), :].astype(F32)
        if reps > 1:
            bias = jnp.concatenate([bias] * reps, axis=1)
        for n in range(N_KV):
            s = _dot_nt(k_blk(n, off), qs_ref[n]) + bias
            m_old = m_ref[n]
            m_new = jnp.maximum(m_old, jnp.max(s, axis=0, keepdims=True))
            p = jnp.exp(s - m_new).astype(BF16)
            acc_ref[n] = acc_ref[n] * jnp.exp(m_old - m_new) + _dot(vt_blk(n, off), p)
            m_ref[n] = m_new
        return 0

    lax.fori_loop(0, n_trips, body, 0)


def _dsa_prompt_body(qi_ref, q_ref, g_ref, ki_ref, kg_ref, vt_ref, o_ref,
                     key_ref, pos_ref, bias_ref, qs_ref, acc_ref, m_ref, *, tq, k_sel):
    j = pl.program_id(1)
    n_keys = (j + 1) * tq
    n_trips = (n_keys + TRIP - 1) // TRIP

    qis = qi_ref[...].reshape(IDX_HEADS * tq, IDX_DIM)
    qs_ref[0] = q_ref[0:GROUP].reshape(GROUP * tq, HEAD_DIM)
    qs_ref[1] = q_ref[GROUP:2 * GROUP].reshape(GROUP * tq, HEAD_DIM)
    gt = g_ref[...].T
    w_row = jnp.concatenate([gt[G_WI + h:G_WI + h + 1, :] for h in range(IDX_HEADS)], axis=1) * (IDX_HEADS ** -0.5)
    qpos = j * tq + lax.broadcasted_iota(I32, (1, LANES), 1)
    limit_row = (qpos // CHUNK + 1) * CHUNK

    def score(i, _):
        off = pl.multiple_of(i * TRIP, TRIP)
        key_ref[pl.ds(off, TRIP), :] = _score_trip(ki_ref[pl.ds(off, TRIP), :], qis, w_row, limit_row, off, tq)
        return 0

    lax.fori_loop(0, n_trips, score, 0)
    _select(key_ref, pos_ref, bias_ref, n_trips, k_sel)
    _attend(bias_ref, n_trips,
            lambda n, off: kg_ref[n, pl.ds(off, TRIP), :],
            lambda n, off: vt_ref[n, :, pl.ds(off, TRIP)],
            qs_ref, acc_ref, m_ref, reps=GROUP)

    for n in range(N_KV):
        a = acc_ref[n]
        o_t = a[0:HEAD_DIM, :] / a[HEAD_DIM:HEAD_DIM + 1, :]
        for pair in range(GROUP // 2):
            two = jnp.concatenate([o_t[:, (2 * pair) * tq:(2 * pair + 1) * tq],
                                   o_t[:, (2 * pair + 1) * tq:(2 * pair + 2) * tq]], axis=0)
            col = (n * GROUP + 2 * pair) * HEAD_DIM
            o_ref[:, col:col + 2 * HEAD_DIM] = two.T


def _dsa_prompt(p, batch, seq, k_sel, *, tq=LANES):
    nq = seq // tq
    return pl.pallas_call(
        functools.partial(_dsa_prompt_body, tq=tq, k_sel=k_sel),
        grid=(batch, nq),
        in_specs=[
            pl.BlockSpec((IDX_HEADS, tq, IDX_DIM), lambda b, j: (0, b * nq + j, 0)),
            pl.BlockSpec((N_HEADS, tq, HEAD_DIM), lambda b, j: (0, b * nq + j, 0)),
            pl.BlockSpec((tq, LANES), lambda b, j: (b * nq + j, 0)),
            pl.BlockSpec((seq, IDX_DIM), lambda b, j: (b, 0)),
            pl.BlockSpec((N_KV, seq, HEAD_DIM), lambda b, j: (0, b, 0)),
            pl.BlockSpec((N_KV, 2 * HEAD_DIM, seq), lambda b, j: (0, 0, b)),
        ],
        out_specs=pl.BlockSpec((tq, N_HEADS * HEAD_DIM), lambda b, j: (b * nq + j, 0)),
        out_shape=jax.ShapeDtypeStruct((batch * seq, N_HEADS * HEAD_DIM), F32),
        scratch_shapes=[
            pltpu.VMEM((seq, LANES), I32), pltpu.VMEM((seq, LANES), I32), pltpu.VMEM((seq, LANES), BF16),
            pltpu.VMEM((N_KV, GROUP * tq, HEAD_DIM), BF16),
            pltpu.VMEM((N_KV, 2 * HEAD_DIM, GROUP * tq), F32),
            pltpu.VMEM((N_KV, 1, GROUP * tq), F32),
        ],
        compiler_params=_params(("arbitrary", "arbitrary")),
        name="dsa_prompt",
    )(p["qi"], p["q"], p["gt"], p["kib"], p["kg"], p["vt"])


def _dsa_sample_body(qi_ref, q_ref, g_ref, kin_ref, kn_ref, vn_ref, kc_ref, vc_ref, ic_ref, o_ref,
                     key_ref, pos_ref, bias_ref, kib_scr, kg_scr, vt_scr, qs_ref, acc_ref, m_ref,
                     *, tq, past, k_sel):
    l_pad = key_ref.shape[0]
    n_trips = l_pad // TRIP
    n_keys = past + tq

    fill = 512
    for c in range(past // fill):
        rows = slice(c * fill, (c + 1) * fill)
        kc = kc_ref[rows, :]
        vt = vc_ref[rows, :].T
        kib_scr[rows, :] = ic_ref[rows, :].astype(BF16)
        for n in range(N_KV):
            kg_scr[n, rows, :] = kc[:, n * HEAD_DIM:(n + 1) * HEAD_DIM].astype(BF16)
            vt_scr[n, 0:HEAD_DIM, rows] = vt[n * HEAD_DIM:(n + 1) * HEAD_DIM, :].astype(BF16)
    tail = l_pad - past
    kn = jnp.concatenate([kn_ref[...], jnp.zeros((tail - tq, N_KV * HEAD_DIM), F32)], axis=0)
    vn = jnp.concatenate([vn_ref[...], jnp.zeros((tail - tq, N_KV * HEAD_DIM), F32)], axis=0).T
    kib_scr[past:l_pad, :] = jnp.concatenate(
        [kin_ref[...], jnp.zeros((tail - tq, IDX_DIM), F32)], axis=0).astype(BF16)
    for n in range(N_KV):
        kg_scr[n, past:l_pad, :] = kn[:, n * HEAD_DIM:(n + 1) * HEAD_DIM].astype(BF16)
        vt_scr[n, 0:HEAD_DIM, past:l_pad] = vn[n * HEAD_DIM:(n + 1) * HEAD_DIM, :].astype(BF16)
        vt_scr[n, HEAD_DIM:2 * HEAD_DIM, :] = jnp.ones((HEAD_DIM, l_pad), BF16)

    qis = qi_ref[...].reshape(IDX_HEADS * tq, IDX_DIM)
    qs_ref[0] = q_ref[0:GROUP].reshape(GROUP * tq, HEAD_DIM)
    qs_ref[1] = q_ref[GROUP:2 * GROUP].reshape(GROUP * tq, HEAD_DIM)
    eye = jnp.where(lax.broadcasted_iota(I32, (LANES, LANES), 0) == lax.broadcasted_iota(I32, (LANES, LANES), 1),
                    1.0, 0.0)
    gt = lax.dot_general(eye, g_ref[...], (((1,), (1,)), ((), ())),
                         precision=lax.Precision.HIGHEST, preferred_element_type=F32)
    w_row = jnp.concatenate([gt[G_WI + h:G_WI + h + 1, :] for h in range(IDX_HEADS)], axis=1) * (IDX_HEADS ** -0.5)
    limit_row = jnp.full((1, LANES), n_keys, I32)

    for i in range(n_trips):
        off = i * TRIP
        key_ref[off:off + TRIP, :] = _score_trip(kib_scr[off:off + TRIP, :], qis, w_row, limit_row, off, tq)
    _select(key_ref, pos_ref, bias_ref, n_trips, k_sel)
    _attend(bias_ref, n_trips,
            lambda n, off: kg_scr[n, pl.ds(off, TRIP), :],
            lambda n, off: vt_scr[n, :, pl.ds(off, TRIP)],
            qs_ref, acc_ref, m_ref, reps=1)

    for n in range(N_KV):
        a = acc_ref[n]
        o = (a[0:HEAD_DIM, :] / a[HEAD_DIM:HEAD_DIM + 1, :]).T
        for pair in range(GROUP // 2):
            col = (n * GROUP + 2 * pair) * HEAD_DIM
            o_ref[:, col:col + 2 * HEAD_DIM] = jnp.concatenate(
                [o[(2 * pair) * tq:(2 * pair + 1) * tq, :], o[(2 * pair + 1) * tq:(2 * pair + 2) * tq, :]], axis=1)


def _dsa_sample(p, row0, streams, tq, cache_k, cache_v, cache_ik, k_sel):
    past = cache_k.shape[1]
    l_pad = ((past + tq + TRIP - 1) // TRIP) * TRIP
    b0 = row0 // tq
    return pl.pallas_call(
        functools.partial(_dsa_sample_body, tq=tq, past=past, k_sel=k_sel),
        grid=(streams,),
        in_specs=[
            pl.BlockSpec((IDX_HEADS, tq, IDX_DIM), lambda s: (0, b0 + s, 0)),
            pl.BlockSpec((N_HEADS, tq, HEAD_DIM), lambda s: (0, b0 + s, 0)),
            pl.BlockSpec((tq, LANES), lambda s: (b0 + s, 0)),
            pl.BlockSpec((tq, IDX_DIM), lambda s: (b0 + s, 0)),
            pl.BlockSpec((tq, N_KV * HEAD_DIM), lambda s: (b0 + s, 0)),
            pl.BlockSpec((tq, N_KV * HEAD_DIM), lambda s: (b0 + s, 0)),
            pl.BlockSpec((None, past, N_KV * HEAD_DIM), lambda s: (s, 0, 0)),
            pl.BlockSpec((None, past, N_KV * HEAD_DIM), lambda s: (s, 0, 0)),
            pl.BlockSpec((None, past, IDX_DIM), lambda s: (s, 0, 0)),
        ],
        out_specs=pl.BlockSpec((tq, N_HEADS * HEAD_DIM), lambda s: (s, 0)),
        out_shape=jax.ShapeDtypeStruct((streams * tq, N_HEADS * HEAD_DIM), F32),
        scratch_shapes=[
            pltpu.VMEM((l_pad, LANES), I32), pltpu.VMEM((l_pad, LANES), I32), pltpu.VMEM((l_pad, LANES), BF16),
            pltpu.VMEM((l_pad, IDX_DIM), BF16),
            pltpu.VMEM((N_KV, l_pad, HEAD_DIM), BF16),
            pltpu.VMEM((N_KV, 2 * HEAD_DIM, l_pad), BF16),
            pltpu.VMEM((N_KV, GROUP * tq, HEAD_DIM), BF16),
            pltpu.VMEM((N_KV, 2 * HEAD_DIM, GROUP * tq), F32),
            pltpu.VMEM((N_KV, 1, GROUP * tq), F32),
        ],
        compiler_params=_params(("arbitrary",)),
        name="dsa_sample",
    )(p["qi"], p["q"], p["gt"], p["kif"], p["kf"], p["vf"], cache_k, cache_v, cache_ik)


def _log_sigmoid(x):
    return jnp.minimum(x, 0.0) - jnp.log1p(jnp.exp(-jnp.abs(x)))


def _mlstm_body(mq_ref, mk_ref, mv_ref, g_ref, bias_ref, c0_ref, n0_ref, m0_ref,
                h_ref, c_ref, n_ref, m_ref, *, t):
    @pl.when(pl.program_id(1) == 0)
    def _():
        c_ref[...] = c0_ref[...]
        n_ref[...] = n0_ref[...]
        m_ref[...] = m0_ref[...]

    gb = g_ref[...] + bias_ref[0:1, :]
    eye = jnp.where(lax.broadcasted_iota(I32, (LANES, LANES), 0) == lax.broadcasted_iota(I32, (LANES, LANES), 1),
                    1.0, 0.0)
    gbt = lax.dot_general(eye, gb, (((1,), (1,)), ((), ())),
                          precision=lax.Precision.HIGHEST, preferred_element_type=F32)
    r = lax.broadcasted_iota(I32, (t, t), 0)
    c = lax.broadcasted_iota(I32, (t, t), 1)
    causal = c <= r
    for h in range(M_HEADS):
        ig_row = gbt[G_MI + h:G_MI + h + 1, :]
        lf_row = _log_sigmoid(gbt[G_MF + h:G_MF + h + 1, :])
        ig_col = gb[:, G_MI + h:G_MI + h + 1]
        lf_col = _log_sigmoid(gb[:, G_MF + h:G_MF + h + 1])
        b_col = jnp.sum(jnp.where(causal, lf_row, 0.0), axis=1, keepdims=True)
        b_row = jnp.sum(jnp.where(r <= c, lf_col, 0.0), axis=0, keepdims=True)
        b_last = b_col[t - 1:t, :]
        m_prev = m_ref[0, h:h + 1, 0:1]
        d = jnp.where(causal, b_col - b_row + ig_row, -jnp.inf)
        m_inter = b_col + m_prev
        m_t = jnp.maximum(m_inter, jnp.max(d, axis=1, keepdims=True))
        sl = slice(h * M_DIM, (h + 1) * M_DIM)
        q = mq_ref[:, sl]
        k = mk_ref[:, sl] * (M_DIM ** -0.5)
        v = mv_ref[:, sl]
        qb, kb = q.astype(BF16), k.astype(BF16)
        sc = _dot_nt(qb, kb) * jnp.exp(d - m_t)
        decay = jnp.exp(m_inter - m_t)
        c_old = c_ref[0, h]
        n_old = n_ref[0, h:h + 1, :]
        num = _dot(sc.astype(BF16), v.astype(BF16)) + decay * _dot_nt(qb, c_old.astype(BF16))
        den = jnp.sum(sc, axis=1, keepdims=True) + decay * jnp.sum(q * n_old, axis=1, keepdims=True)
        h_ref[:, sl] = num / jnp.maximum(jnp.abs(den), jnp.exp(-m_t))
        m_new = m_t[t - 1:t, :]
        g_col = jnp.exp(b_last - b_col + ig_col - m_new)
        dec_end = jnp.exp(b_last + m_prev - m_new)
        vg_t = (v * g_col).T.astype(BF16)
        c_ref[0, h] = dec_end * c_old + _dot(vg_t, kb)
        n_ref[0, h:h + 1, :] = dec_end * n_old + jnp.sum(g_col * k, axis=0, keepdims=True)
        m_ref[0, h:h + 1, :] = jnp.broadcast_to(m_new, (1, LANES))


def _mlstm(p, row0, batch, seq, t, bias, c0, n0, m0):
    nc = seq // t
    b0 = row0 // t
    hm = M_HEADS * M_DIM
    tok = lambda b, c: (b0 + b * nc + c, 0)
    st4 = lambda b, c: (b, 0, 0, 0)
    st3 = lambda b, c: (b, 0, 0)
    return pl.pallas_call(
        functools.partial(_mlstm_body, t=t),
        grid=(batch, nc),
        in_specs=[
            pl.BlockSpec((t, hm), tok), pl.BlockSpec((t, hm), tok), pl.BlockSpec((t, hm), tok),
            pl.BlockSpec((t, LANES), tok),
            pl.BlockSpec((8, LANES), lambda b, c: (0, 0)),
            pl.BlockSpec((1, M_HEADS, M_DIM, M_DIM), st4),
            pl.BlockSpec((1, M_HEADS, M_DIM), st3),
            pl.BlockSpec((1, M_HEADS, LANES), st3),
        ],
        out_specs=[
            pl.BlockSpec((t, hm), lambda b, c: (b * nc + c, 0)),
            pl.BlockSpec((1, M_HEADS, M_DIM, M_DIM), st4),
            pl.BlockSpec((1, M_HEADS, M_DIM), st3),
            pl.BlockSpec((1, M_HEADS, LANES), st3),
        ],
        out_shape=[
            jax.ShapeDtypeStruct((batch * seq, hm), F32),
            jax.ShapeDtypeStruct((batch, M_HEADS, M_DIM, M_DIM), F32),
            jax.ShapeDtypeStruct((batch, M_HEADS, M_DIM), F32),
            jax.ShapeDtypeStruct((batch, M_HEADS, LANES), F32),
        ],
        compiler_params=_params(("arbitrary", "arbitrary")),
        name="mlstm",
    )(p["mq"], p["mk"], p["mv"], p["gt"], bias, c0, n0, m0)


def _merge_body(x_ref, attn_ref, mh_ref, mo_ref, ga_ref, gm_ref, gml_ref, wap_ref, wmp_ref, wo_ref, o_ref):
    a = _dot(attn_ref[...].astype(BF16), wap_ref[...])
    parts = []
    for h in range(M_HEADS):
        sl = slice(h * M_DIM, (h + 1) * M_DIM)
        parts.append(_rms(mh_ref[:, sl], gml_ref[:, sl]))
    hn = jnp.concatenate(parts, axis=1)
    b = _dot((jax.nn.sigmoid(mo_ref[...]) * hn).astype(BF16), wmp_ref[...])
    mix = jax.nn.sigmoid(ga_ref[...]) * a + jax.nn.sigmoid(gm_ref[...]) * b
    o_ref[...] = x_ref[...] + _dot(mix.astype(BF16), wo_ref[...])


def _merge(x, attn, mh, p, g_ml, w_ap, w_mp, w_o, *, tm=256):
    n, d = x.shape
    tm = _tile(n, tm)
    ha, hm = N_HEADS * HEAD_DIM, M_HEADS * M_DIM
    row = lambda i: (i, 0)
    fix = lambda i: (0, 0)
    return pl.pallas_call(
        _merge_body,
        grid=(n // tm,),
        in_specs=[
            pl.BlockSpec((tm, d), row), pl.BlockSpec((tm, ha), row), pl.BlockSpec((tm, hm), row),
            pl.BlockSpec((tm, hm), row), pl.BlockSpec((tm, d), row), pl.BlockSpec((tm, d), row),
            pl.BlockSpec((1, hm), fix), pl.BlockSpec((ha, d), fix), pl.BlockSpec((hm, d), fix),
            pl.BlockSpec((d, d), fix),
        ],
        out_specs=pl.BlockSpec((tm, d), row),
        out_shape=jax.ShapeDtypeStruct((n, d), F32),
        compiler_params=_params(("parallel",)),
        name="merge",
    )(x, attn, mh, p["mo"], p["ga"], p["gm"], g_ml.reshape(1, hm), w_ap, w_mp, w_o)


def _rearrange_w_in(w):
    cuts = [int(c) for c in np.cumsum(SPLIT_SIZES)[:-1]]
    q, k, v, qi, ki, wi, mq, mk, mv, mi, mf, mo, ga, gm = jnp.split(w, cuts, axis=1)
    pad = jnp.zeros((w.shape[0], LANES - IDX_DIM - 3 * IDX_HEADS), w.dtype)
    return jnp.concatenate([q, k, v, qi, ki, wi, mi, mf, pad, mq, mk, mv, mo, ga, gm], axis=1).astype(BF16)


def _rope_tables(pos):
    freqs = ROPE_THETA ** (-jnp.arange(ROT_HALF, dtype=F32) / ROT_HALF)
    ang = pos.astype(F32)[:, None] * freqs[None, :]
    cos, sin = jnp.cos(ang), jnp.sin(ang)
    n = pos.shape[0]
    one = jnp.ones((n, HEAD_DIM - 2 * ROT_HALF), F32)
    zero8 = jnp.zeros((n, ROT_HALF), F32)
    zero = jnp.zeros((n, HEAD_DIM - 2 * ROT_HALF), F32)
    two = lambda a: jnp.concatenate([a, a], axis=1)
    return (two(jnp.concatenate([cos, cos, one], axis=1)),
            two(jnp.concatenate([zero8, sin, zero], axis=1)),
            two(jnp.concatenate([-sin, zero8, zero], axis=1)))


def kernel(x_prompt, x_sample, cache_k, cache_v, cache_idx_k, state_C, state_n, state_m,
           g_ffn1, w_ffn1_gate, w_ffn1_up, w_ffn1_down, g_mix, w_in, b_igate, b_fgate,
           g_mlstm, w_attn_proj, w_mlstm_proj, w_out, g_ffn2, w_ffn2_gate, w_ffn2_up,
           w_ffn2_down, g_final):
    batch, seq, d = x_prompt.shape
    streams, t_s, _ = x_sample.shape
    depth = w_in.shape[0]
    past = cache_k.shape[2]
    n_p, n_s = batch * seq, streams * t_s
    proj_tm = 256
    t_prompt = 256

    x = jnp.concatenate([x_prompt.reshape(n_p, d), x_sample.reshape(n_s, d)], axis=0)

    pos = jnp.concatenate([jnp.arange(seq, dtype=I32),
                           past + (jnp.arange(proj_tm, dtype=I32) % t_s)])
    cos_t, s1_t, s2_t = _rope_tables(pos)
    tiles_per_seq = seq // proj_tm
    n_prompt_tiles = n_p // proj_tm
    tab_block = lambda i: jnp.where(i < n_prompt_tiles, i % tiles_per_seq, tiles_per_seq)

    k_sel_p = min(TOPK_MAX, seq // 4)
    k_sel_s = min(TOPK_MAX, (past + t_s) // 4)

    kp, vp, ikp, cp, np_, mp = [], [], [], [], [], []
    ks, vs, iks, cs, ns, ms = [], [], [], [], [], []
    for l in range(depth):
        last = l == depth - 1
        x = _ffn(x, g_ffn1[l], w_ffn1_gate[l].astype(BF16), w_ffn1_up[l].astype(BF16),
                 w_ffn1_down[l].astype(BF16), g_final, final_norm=False)
        p = _proj(x, g_mix[l], _rearrange_w_in(w_in[l]), cos_t, s1_t, s2_t, tab_block, tm=proj_tm)

        attn_p = _dsa_prompt(p, batch, seq, k_sel_p)
        attn_s = _dsa_sample(p, n_p, streams, t_s,
                             cache_k[l].reshape(streams, past, N_KV * HEAD_DIM),
                             cache_v[l].reshape(streams, past, N_KV * HEAD_DIM),
                             cache_idx_k[l], k_sel_s)

        bias = jnp.zeros((8, LANES), F32)
        bias = bias.at[0, G_MI:G_MI + M_HEADS].set(b_igate[l]).at[0, G_MF:G_MF + M_HEADS].set(b_fgate[l])
        mh_p, c_p, n_pr, m_p = _mlstm(
            p, 0, batch, seq, t_prompt, bias,
            jnp.zeros((batch, M_HEADS, M_DIM, M_DIM), F32), jnp.zeros((batch, M_HEADS, M_DIM), F32),
            jnp.full((batch, M_HEADS, LANES), M_NEG, F32))
        mh_s, c_s, n_sm, m_s = _mlstm(
            p, n_p, streams, t_s, t_s, bias, state_C[l], state_n[l],
            jnp.broadcast_to(state_m[l][:, :, None], (streams, M_HEADS, LANES)))

        x = _merge(x, jnp.concatenate([attn_p, attn_s], axis=0), jnp.concatenate([mh_p, mh_s], axis=0), p,
                   g_mlstm[l], w_attn_proj[l].astype(BF16), w_mlstm_proj[l].astype(BF16), w_out[l].astype(BF16))
        x = _ffn(x, g_ffn2[l], w_ffn2_gate[l].astype(BF16), w_ffn2_up[l].astype(BF16),
                 w_ffn2_down[l].astype(BF16), g_final, final_norm=last)

        kp.append(p["kf"][:n_p].reshape(batch, seq, N_KV, HEAD_DIM))
        vp.append(p["vf"][:n_p].reshape(batch, seq, N_KV, HEAD_DIM))
        ikp.append(p["kif"][:n_p].reshape(batch, seq, IDX_DIM))
        cp.append(c_p); np_.append(n_pr); mp.append(m_p[:, :, 0])
        ks.append(p["kf"][n_p:].reshape(streams, t_s, N_KV, HEAD_DIM))
        vs.append(p["vf"][n_p:].reshape(streams, t_s, N_KV, HEAD_DIM))
        iks.append(p["kif"][n_p:].reshape(streams, t_s, IDX_DIM))
        cs.append(c_s); ns.append(n_sm); ms.append(m_s[:, :, 0])

    st = jnp.stack
    return (x[:n_p].reshape(batch, seq, d), x[n_p:].reshape(streams, t_s, d),
            st(kp), st(vp), st(ikp), st(cp), st(np_), st(mp),
            st(ks), st(vs), st(iks), st(cs), st(ns), st(ms))
```

```python
import functools

import jax
import jax.numpy as jnp
import numpy as np
from jax import lax
from jax.experimental import pallas as pl
from jax.experimental.pallas import tpu as pltpu

F32 = jnp.float32
BF16 = jnp.bfloat16
I32 = jnp.int32

D_MODEL = 1024
D_FF = 2816
N_HEADS = 8
N_KV = 2
GROUP = N_HEADS // N_KV
HEAD_DIM = 64
IDX_HEADS = 4
IDX_DIM = 64
ROT_HALF = 8
ROPE_THETA = 500000.0
M_HEADS = 4
M_DIM = 128
CHUNK = 64
TOPK_MAX = 256
EPS = 1e-6
M_NEG = -1e30
PAST_LEN = 4096
SPLIT_SIZES = (N_HEADS * HEAD_DIM, N_KV * HEAD_DIM, N_KV * HEAD_DIM,
               IDX_HEADS * IDX_DIM, IDX_DIM, IDX_HEADS,
               M_HEADS * M_DIM, M_HEADS * M_DIM, M_HEADS * M_DIM, M_HEADS, M_HEADS, M_HEADS * M_DIM,
               D_MODEL, D_MODEL)

LANES = 128
INT_MIN = -2 ** 31
NEG_BIG = -1e30
VMEM_LIMIT = 48 * 1024 * 1024

OFF_Q, OFF_K, OFF_V, OFF_QI, OFF_G = 0, 512, 640, 768, 1024
OFF_MQ, OFF_MK, OFF_MV, OFF_MO, OFF_GA, OFF_GM = 1152, 1664, 2176, 2688, 3200, 4224
PROJ_COLS = 5248
G_WI, G_MI, G_MF = 64, 68, 72


def _params(sem):
    return pltpu.CompilerParams(dimension_semantics=sem, vmem_limit_bytes=VMEM_LIMIT)


def _dot(a, b):
    return jnp.dot(a, b, preferred_element_type=F32)


def _dot_nt(a, b):
    return lax.dot_general(a, b, (((1,), (1,)), ((), ())), preferred_element_type=F32)


def _tile(n, pref):
    while n % pref:
        pref //= 2
    return pref


def _rms(x, g):
    return x * lax.rsqrt(jnp.mean(x * x, axis=-1, keepdims=True) + EPS) * g


def _ffn_body(x_ref, g_ref, wg_ref, wu_ref, wd_ref, gf_ref, *rest, final_norm, split_tile):
    outs, (h_scr, acc_scr) = rest[:-2], rest[-2:]
    f = pl.program_id(1)

    @pl.when(f == 0)
    def _():
        h_scr[...] = _rms(x_ref[...], g_ref[...]).astype(BF16)
        acc_scr[...] = jnp.zeros_like(acc_scr)

    h = h_scr[...]
    gate = _dot(h, wg_ref[...])
    up = _dot(h, wu_ref[...])
    act = (gate * jax.nn.sigmoid(gate) * up).astype(BF16)
    acc_scr[...] += _dot(act, wd_ref[...])

    @pl.when(f == pl.num_programs(1) - 1)
    def _():
        y = x_ref[...] + 0.5 * acc_scr[...]
        if final_norm:
            y = _rms(y, gf_ref[...])
        if split_tile is None:
            outs[0][...] = y
        else:
            @pl.when(pl.program_id(0) < split_tile)
            def _():
                outs[0][...] = y

            @pl.when(pl.program_id(0) >= split_tile)
            def _():
                outs[1][...] = y


def _ffn(x, g, wg, wu, wd, g_final, *, final_norm, split_rows=None, tm=512, tf=1408):
    n, d = x.shape
    dff = wg.shape[1]
    tm = _tile(n, tm)
    assert dff % tf == 0
    if split_rows is None:
        split_tile = None
        out_specs = pl.BlockSpec((tm, d), lambda i, f: (i, 0))
        out_shape = jax.ShapeDtypeStruct((n, d), F32)
    else:
        assert split_rows % tm == 0
        split_tile = split_rows // tm
        out_specs = [pl.BlockSpec((tm, d), lambda i, f: (jnp.minimum(i, split_tile - 1), 0)),
                     pl.BlockSpec((tm, d), lambda i, f: (jnp.maximum(i - split_tile, 0), 0))]
        out_shape = [jax.ShapeDtypeStruct((split_rows, d), F32), jax.ShapeDtypeStruct((n - split_rows, d), F32)]
    return pl.pallas_call(
        functools.partial(_ffn_body, final_norm=final_norm, split_tile=split_tile),
        grid=(n // tm, dff // tf),
        in_specs=[
            pl.BlockSpec((tm, d), lambda i, f: (i, 0)),
            pl.BlockSpec((1, d), lambda i, f: (0, 0)),
            pl.BlockSpec((d, tf), lambda i, f: (0, f)),
            pl.BlockSpec((d, tf), lambda i, f: (0, f)),
            pl.BlockSpec((tf, d), lambda i, f: (f, 0)),
            pl.BlockSpec((1, d), lambda i, f: (0, 0)),
        ],
        out_specs=out_specs,
        out_shape=out_shape,
        scratch_shapes=[pltpu.VMEM((tm, d), BF16), pltpu.VMEM((tm, d), F32)],
        compiler_params=_params(("arbitrary", "arbitrary")),
        name="ffn",
    )(x, g.reshape(1, d), wg, wu, wd, g_final.reshape(1, d))


def _key_operands(k):
    lane = lax.broadcasted_iota(I32, k.shape, 1)
    tail = jnp.where(lane == HEAD_DIM, 1.0, 0.0)
    return [jnp.where(lane < HEAD_DIM, kn, tail).astype(BF16) for kn in (k, pltpu.roll(k, HEAD_DIM, 1))]


def _rope_tile(x, c, s1, s2):
    return x * c + pltpu.roll(x, ROT_HALF, 1) * s1 + pltpu.roll(x, LANES - ROT_HALF, 1) * s2


def _proj_body(x_ref, g_ref, w_ref, c_ref, s1_ref, s2_ref,
               q_ref, kg_ref, kf_ref, vt_ref, vf_ref, qi_ref, kib_ref, kif_ref, gt_ref,
               mq_ref, mk_ref, mv_ref, mo_ref, ga_ref, gm_ref):
    h = _rms(x_ref[...], g_ref[...]).astype(BF16)
    c, s1, s2 = c_ref[...], s1_ref[...], s2_ref[...]

    def seg(off, width):
        return _dot(h, w_ref[:, off:off + width])

    def rope(z):
        tiles = [_rope_tile(z[:, t * LANES:(t + 1) * LANES], c, s1, s2) for t in range(z.shape[1] // LANES)]
        return tiles[0] if len(tiles) == 1 else jnp.concatenate(tiles, axis=1)

    q = rope(seg(OFF_Q, N_HEADS * HEAD_DIM)) * (HEAD_DIM ** -0.5)
    for hd in range(N_HEADS):
        q_ref[hd] = q[:, hd * HEAD_DIM:(hd + 1) * HEAD_DIM].astype(BF16)

    k = rope(seg(OFF_K, N_KV * HEAD_DIM))
    kf_ref[...] = k
    for n, kn in enumerate(_key_operands(k)):
        kg_ref[n] = kn

    v = seg(OFF_V, N_KV * HEAD_DIM)
    vf_ref[...] = v
    vt = v.T
    ones = jnp.ones((HEAD_DIM, vt.shape[1]), BF16)
    for n in range(N_KV):
        vt_ref[n, 0:HEAD_DIM, :] = vt[n * HEAD_DIM:(n + 1) * HEAD_DIM, :].astype(BF16)
        vt_ref[n, HEAD_DIM:2 * HEAD_DIM, :] = ones

    qi = rope(seg(OFF_QI, IDX_HEADS * IDX_DIM)) * (IDX_DIM ** -0.5)
    for hd in range(IDX_HEADS):
        qi_ref[hd] = qi[:, hd * IDX_DIM:(hd + 1) * IDX_DIM].astype(BF16)

    gz = seg(OFF_G, LANES)
    lane = lax.broadcasted_iota(I32, gz.shape, 1)
    is_ki = lane < IDX_DIM
    gz = _rope_tile(gz, jnp.where(is_ki, c, 1.0), jnp.where(is_ki, s1, 0.0), jnp.where(is_ki, s2, 0.0))
    gt_ref[...] = gz
    kif_ref[...] = gz[:, 0:IDX_DIM]
    kib_ref[...] = gz[:, 0:IDX_DIM].astype(BF16)

    mq_ref[...] = seg(OFF_MQ, M_HEADS * M_DIM)
    mk_ref[...] = seg(OFF_MK, M_HEADS * M_DIM)
    mv_ref[...] = seg(OFF_MV, M_HEADS * M_DIM)
    mo_ref[...] = seg(OFF_MO, M_HEADS * M_DIM)
    ga_ref[...] = seg(OFF_GA, D_MODEL)
    gm_ref[...] = seg(OFF_GM, D_MODEL)


def _proj(x, g, w, cos_t, s1_t, s2_t, tab_block, *, tm=256):
    n, d = x.shape
    assert n % tm == 0
    hm = M_HEADS * M_DIM
    row = lambda i: (i, 0)
    tab = lambda i: (tab_block(i), 0)
    out_shapes = dict(
        q=((N_HEADS, n, HEAD_DIM), BF16, pl.BlockSpec((N_HEADS, tm, HEAD_DIM), lambda i: (0, i, 0))),
        kg=((N_KV, n, LANES), BF16, pl.BlockSpec((N_KV, tm, LANES), lambda i: (0, i, 0))),
        kf=((n, N_KV * HEAD_DIM), F32, pl.BlockSpec((tm, N_KV * HEAD_DIM), row)),
        vt=((N_KV, 2 * HEAD_DIM, n), BF16, pl.BlockSpec((N_KV, 2 * HEAD_DIM, tm), lambda i: (0, 0, i))),
        vf=((n, N_KV * HEAD_DIM), F32, pl.BlockSpec((tm, N_KV * HEAD_DIM), row)),
        qi=((IDX_HEADS, n, IDX_DIM), BF16, pl.BlockSpec((IDX_HEADS, tm, IDX_DIM), lambda i: (0, i, 0))),
        kib=((n, IDX_DIM), BF16, pl.BlockSpec((tm, IDX_DIM), row)),
        kif=((n, IDX_DIM), F32, pl.BlockSpec((tm, IDX_DIM), row)),
        gt=((n, LANES), F32, pl.BlockSpec((tm, LANES), row)),
        mq=((n, hm), F32, pl.BlockSpec((tm, hm), row)),
        mk=((n, hm), F32, pl.BlockSpec((tm, hm), row)),
        mv=((n, hm), F32, pl.BlockSpec((tm, hm), row)),
        mo=((n, hm), F32, pl.BlockSpec((tm, hm), row)),
        ga=((n, D_MODEL), F32, pl.BlockSpec((tm, D_MODEL), row)),
        gm=((n, D_MODEL), F32, pl.BlockSpec((tm, D_MODEL), row)),
    )
    names = list(out_shapes)
    outs = pl.pallas_call(
        _proj_body,
        grid=(n // tm,),
        in_specs=[
            pl.BlockSpec((tm, d), row),
            pl.BlockSpec((1, d), lambda i: (0, 0)),
            pl.BlockSpec((d, PROJ_COLS), lambda i: (0, 0)),
            pl.BlockSpec((tm, LANES), tab),
            pl.BlockSpec((tm, LANES), tab),
            pl.BlockSpec((tm, LANES), tab),
        ],
        out_specs=[out_shapes[k][2] for k in names],
        out_shape=[jax.ShapeDtypeStruct(out_shapes[k][0], out_shapes[k][1]) for k in names],
        compiler_params=_params(("parallel",)),
        name="proj",
    )(x, g.reshape(1, d), w, cos_t, s1_t, s2_t)
    return dict(zip(names, outs))


TRIP = 512
CACHE_FILL = 512
SAFE_SHIFT = 40.0
SUBL = 8
REP = (SUBL, LANES)


def _sortable(score):
    score = jnp.where(score == 0.0, 0.0, score)
    bits = lax.bitcast_convert_type(score, I32)
    return bits ^ ((bits >> 31) & 0x7FFFFFFF)


def _head_sum(x, tq):
    if tq == LANES:
        return x[:, 0:128] + x[:, 128:256] + x[:, 256:384] + x[:, 384:512]
    assert 4 * tq == LANES
    return x + pltpu.roll(x, tq, 1) + pltpu.roll(x, 2 * tq, 1) + pltpu.roll(x, 3 * tq, 1)


def _score_trip(ki_blk, qis, w_row, limit_row, off, tq):
    lg = jnp.maximum(_dot_nt(ki_blk, qis), 0.0) * w_row
    key = _sortable(_head_sum(lg, tq))
    pos = off + lax.broadcasted_iota(I32, key.shape, 0)
    return jnp.where(pos < limit_row, key, INT_MIN)


def _all_sublanes(x):
    x = x + pltpu.roll(x, 4, 0)
    x = x + pltpu.roll(x, 2, 0)
    return x + pltpu.roll(x, 1, 0)


def _planes(ref, off, rows):
    return ref[pl.ds(off, rows), :].reshape(rows // SUBL, SUBL, LANES)


def _count_ge(ref, n_trips, cand):
    def body(i, accs):
        t = _planes(ref, pl.multiple_of(i * TRIP, TRIP), TRIP)
        ind = jnp.where(t >= cand, 1, 0)
        accs = list(accs)
        for r in range(TRIP // SUBL):
            accs[r % len(accs)] = accs[r % len(accs)] + ind[r]
        return tuple(accs)

    zero = jnp.zeros(REP, I32)
    a = lax.fori_loop(0, n_trips, body, (zero, zero, zero, zero))
    return _all_sublanes((a[0] + a[1]) + (a[2] + a[3])).astype(F32)


def _kth_largest(ref, n_trips, kf, total):
    c0 = _count_ge(ref, n_trips, jnp.zeros(REP, I32))
    ok = c0 >= kf
    state = (jnp.where(ok, 0, INT_MIN).astype(I32), jnp.where(ok, c0, total), jnp.where(ok, 0.0, c0))

    def bit_pass(p, state):
        base, c_ge, c_gt = state
        cand = base | lax.shift_left(jnp.int32(1), 30 - p)
        c = _count_ge(ref, n_trips, cand)
        ok = c >= kf
        return jnp.where(ok, cand, base), jnp.where(ok, c, c_ge), jnp.where(ok, c_gt, c)

    return lax.fori_loop(0, 31, bit_pass, state)


def _kth_pos(ref, n_trips, kf, nbits):
    def bit_pass(p, base):
        cand = base | lax.shift_left(jnp.int32(1), nbits - 1 - p)
        return jnp.where(_count_ge(ref, n_trips, cand) >= kf, cand, base)

    return lax.fori_loop(0, nbits, bit_pass, jnp.zeros(REP, I32))


def _select(key_ref, pos_ref, bias_ref, n_trips, k_sel):
    l_pad = key_ref.shape[0]
    total = jnp.full(REP, 1.0, F32) * (n_trips * TRIP)
    kf = jnp.full(REP, float(k_sel), F32)
    thr, c_ge, c_gt = _kth_largest(key_ref, n_trips, kf, total)
    ties = kf - c_gt
    shape3 = (TRIP // SUBL, SUBL, LANES)
    row_in_trip = lax.broadcasted_iota(I32, shape3, 0) * SUBL + lax.broadcasted_iota(I32, shape3, 1)

    def fill_pos(i, _):
        off = pl.multiple_of(i * TRIP, TRIP)
        rev = (l_pad - 1 - off) - row_in_trip
        pos_ref[pl.ds(off, TRIP), :] = jnp.where(_planes(key_ref, off, TRIP) == thr, rev, -1).reshape(TRIP, LANES)
        return 0

    lax.fori_loop(0, n_trips, fill_pos, 0)
    short = thr == INT_MIN
    crowded = jnp.where(c_ge - c_gt > ties, jnp.where(short, 0.0, 1.0), 0.0)
    nbits = max(1, (l_pad - 1).bit_length())
    pos_thr = lax.cond(jnp.max(crowded) > 0.0,
                       lambda: _kth_pos(pos_ref, n_trips, ties, nbits),
                       lambda: jnp.zeros(REP, I32))
    pos_thr = jnp.where(short, l_pad, pos_thr)

    def fill_bias(i, _):
        off = pl.multiple_of(i * TRIP, TRIP)
        bias = jnp.where(_planes(key_ref, off, TRIP) > thr, 0.0,
                         jnp.where(_planes(pos_ref, off, TRIP) >= pos_thr, 0.0, NEG_BIG))
        bias_ref[pl.ds(off, TRIP), :] = bias.reshape(TRIP, LANES).astype(BF16)
        return 0

    lax.fori_loop(0, n_trips, fill_bias, 0)


def _key_norm_max(k_rows, n_trips):
    def body(i, m):
        kf = k_rows(i).astype(F32)
        kf = jnp.where(lax.broadcasted_iota(I32, kf.shape, 1) < HEAD_DIM, kf, 0.0)
        n2 = jnp.sum(kf * kf, axis=1, keepdims=True)
        return jnp.maximum(m, jnp.max(n2, axis=0, keepdims=True))

    return jnp.sqrt(lax.fori_loop(0, n_trips, body, jnp.zeros((1, 1), F32)))


def _query_operands(q_ref, kmax_ref, qs_ref, tq):
    qf = q_ref[...].astype(F32)
    qn = jnp.sqrt(jnp.sum(qf * qf, axis=-1, keepdims=True))
    qe = jnp.concatenate([qf, jnp.zeros_like(qf)], axis=-1)
    lane = lax.broadcasted_iota(I32, (GROUP, tq, LANES), 2)
    worst = jnp.zeros((1, 1), F32)
    for n in range(N_KV):
        shift = qn[n * GROUP:(n + 1) * GROUP] * kmax_ref[n, :, 0:1]
        op = jnp.where(lane == HEAD_DIM, -shift, qe[n * GROUP:(n + 1) * GROUP])
        qs_ref[n] = op.astype(BF16).reshape(GROUP * tq, LANES)
        worst = jnp.maximum(worst, jnp.max(jnp.max(shift, axis=1), axis=0, keepdims=True))
    return worst


def _attend(bias_ref, n_trips, k_blk, vt_blk, qs_ref, acc_ref, m_ref, reps, worst_shift):
    acc_ref[...] = jnp.zeros_like(acc_ref)

    def logits(i, n):
        off = pl.multiple_of(i * TRIP, TRIP)
        bias = bias_ref[pl.ds(off, TRIP), :].astype(F32)
        if reps > 1:
            bias = jnp.concatenate([bias] * reps, axis=1)
        return _dot_nt(k_blk(n, off), qs_ref[n]) + bias, off

    def plain():
        def body(i, _):
            for n in range(N_KV):
                s, off = logits(i, n)
                acc_ref[n] += _dot(vt_blk(n, off), jnp.exp(s).astype(BF16))
            return 0

        lax.fori_loop(0, n_trips, body, 0)

    def running_max():
        m_ref[...] = jnp.full(m_ref.shape, NEG_BIG, F32)

        def body(i, _):
            for n in range(N_KV):
                s, off = logits(i, n)
                m_old = m_ref[n]
                m_new = jnp.maximum(m_old, jnp.max(s, axis=0, keepdims=True))
                p = jnp.exp(s - m_new).astype(BF16)
                acc_ref[n] = acc_ref[n] * jnp.exp(m_old - m_new) + _dot(vt_blk(n, off), p)
                m_ref[n] = m_new
            return 0

        lax.fori_loop(0, n_trips, body, 0)

    lax.cond(worst_shift[0, 0] <= SAFE_SHIFT, plain, running_max)


def _dsa_prompt_body(qi_ref, q_ref, g_ref, ki_ref, kg_ref, vt_ref, o_ref,
                     key_ref, pos_ref, bias_ref, qs_ref, acc_ref, m_ref, kmax_ref, *, tq, k_sel):
    j = pl.program_id(1)
    n_keys = (j + 1) * tq
    n_trips = (n_keys + TRIP - 1) // TRIP

    @pl.when(j == 0)
    def _():
        for n in range(N_KV):
            kmax = _key_norm_max(lambda i: kg_ref[n, pl.ds(pl.multiple_of(i * TRIP, TRIP), TRIP), :],
                                 kg_ref.shape[1] // TRIP)
            kmax_ref[n] = jnp.broadcast_to(kmax, (1, LANES))

    qis = qi_ref[...].reshape(IDX_HEADS * tq, IDX_DIM)
    worst_shift = _query_operands(q_ref, kmax_ref, qs_ref, tq)
    gt = g_ref[...].T
    w_row = jnp.concatenate([gt[G_WI + h:G_WI + h + 1, :] for h in range(IDX_HEADS)], axis=1) * (IDX_HEADS ** -0.5)
    qpos = j * tq + lax.broadcasted_iota(I32, (1, LANES), 1)
    limit_row = (qpos // CHUNK + 1) * CHUNK

    def score(i, _):
        off = pl.multiple_of(i * TRIP, TRIP)
        key_ref[pl.ds(off, TRIP), :] = _score_trip(ki_ref[pl.ds(off, TRIP), :], qis, w_row, limit_row, off, tq)
        return 0

    lax.fori_loop(0, n_trips, score, 0)
    _select(key_ref, pos_ref, bias_ref, n_trips, k_sel)
    _attend(bias_ref, n_trips,
            lambda n, off: kg_ref[n, pl.ds(off, TRIP), :],
            lambda n, off: vt_ref[n, :, pl.ds(off, TRIP)],
            qs_ref, acc_ref, m_ref, GROUP, worst_shift)

    for n in range(N_KV):
        a = acc_ref[n]
        o_t = a[0:HEAD_DIM, :] / a[HEAD_DIM:HEAD_DIM + 1, :]
        for pair in range(GROUP // 2):
            two = jnp.concatenate([o_t[:, (2 * pair) * tq:(2 * pair + 1) * tq],
                                   o_t[:, (2 * pair + 1) * tq:(2 * pair + 2) * tq]], axis=0)
            col = (n * GROUP + 2 * pair) * HEAD_DIM
            o_ref[:, col:col + 2 * HEAD_DIM] = two.T


def _dsa_prompt(p, batch, seq, k_sel, *, tq=LANES):
    nq = seq // tq
    n_total = p["q"].shape[1]
    return pl.pallas_call(
        functools.partial(_dsa_prompt_body, tq=tq, k_sel=k_sel),
        grid=(batch, nq),
        in_specs=[
            pl.BlockSpec((IDX_HEADS, tq, IDX_DIM), lambda b, j: (0, b * nq + j, 0)),
            pl.BlockSpec((N_HEADS, tq, HEAD_DIM), lambda b, j: (0, b * nq + j, 0)),
            pl.BlockSpec((tq, LANES), lambda b, j: (b * nq + j, 0)),
            pl.BlockSpec((seq, IDX_DIM), lambda b, j: (b, 0)),
            pl.BlockSpec((N_KV, seq, LANES), lambda b, j: (0, b, 0)),
            pl.BlockSpec((N_KV, 2 * HEAD_DIM, seq), lambda b, j: (0, 0, b)),
        ],
        out_specs=pl.BlockSpec((tq, N_HEADS * HEAD_DIM), lambda b, j: (b * nq + j, 0)),
        out_shape=jax.ShapeDtypeStruct((n_total, N_HEADS * HEAD_DIM), F32),
        scratch_shapes=[
            pltpu.VMEM((seq, LANES), I32), pltpu.VMEM((seq, LANES), I32), pltpu.VMEM((seq, LANES), BF16),
            pltpu.VMEM((N_KV, GROUP * tq, LANES), BF16),
            pltpu.VMEM((N_KV, 2 * HEAD_DIM, GROUP * tq), F32),
            pltpu.VMEM((N_KV, 1, GROUP * tq), F32),
            pltpu.VMEM((N_KV, 1, LANES), F32),
        ],
        compiler_params=_params(("arbitrary", "arbitrary")),
        name="dsa_prompt",
    )(p["qi"], p["q"], p["gt"], p["kib"], p["kg"], p["vt"])


def _dsa_sample_body(qi_ref, q_ref, g_ref, kin_ref, kn_ref, vn_ref, kc_ref, vc_ref, ic_ref, buf_ref, o_ref,
                     key_ref, pos_ref, bias_ref, kib_scr, kg_scr, vt_scr, qs_ref, acc_ref, m_ref, kmax_ref,
                     *, tq, past, k_sel):
    del buf_ref
    l_pad = key_ref.shape[0]
    n_trips = l_pad // TRIP
    n_keys = past + tq

    for c in range(past // CACHE_FILL):
        rows = slice(c * CACHE_FILL, (c + 1) * CACHE_FILL)
        kc = kc_ref[rows, :]
        vt = vc_ref[rows, :].T
        kib_scr[rows, :] = ic_ref[rows, :].astype(BF16)
        for n, kop in enumerate(_key_operands(kc)):
            kg_scr[n, rows, :] = kop
            vt_scr[n, 0:HEAD_DIM, rows] = vt[n * HEAD_DIM:(n + 1) * HEAD_DIM, :].astype(BF16)
    tail = l_pad - past
    kn = jnp.concatenate([kn_ref[...], jnp.zeros((tail - tq, N_KV * HEAD_DIM), F32)], axis=0)
    vn = jnp.concatenate([vn_ref[...], jnp.zeros((tail - tq, N_KV * HEAD_DIM), F32)], axis=0).T
    kib_scr[past:l_pad, :] = jnp.concatenate(
        [kin_ref[...], jnp.zeros((tail - tq, IDX_DIM), F32)], axis=0).astype(BF16)
    for n, kop in enumerate(_key_operands(kn)):
        kg_scr[n, past:l_pad, :] = kop
        vt_scr[n, 0:HEAD_DIM, past:l_pad] = vn[n * HEAD_DIM:(n + 1) * HEAD_DIM, :].astype(BF16)
        vt_scr[n, HEAD_DIM:2 * HEAD_DIM, :] = jnp.ones((HEAD_DIM, l_pad), BF16)

    qis = qi_ref[...].reshape(IDX_HEADS * tq, IDX_DIM)
    kmax_ref[...] = jnp.zeros_like(kmax_ref)
    _query_operands(q_ref, kmax_ref, qs_ref, tq)
    worst_shift = jnp.full((1, 1), jnp.inf, F32)
    eye = jnp.where(lax.broadcasted_iota(I32, (LANES, LANES), 0) == lax.broadcasted_iota(I32, (LANES, LANES), 1),
                    1.0, 0.0)
    gt = lax.dot_general(eye, g_ref[...], (((1,), (1,)), ((), ())),
                         precision=lax.Precision.HIGHEST, preferred_element_type=F32)
    w_row = jnp.concatenate([gt[G_WI + h:G_WI + h + 1, :] for h in range(IDX_HEADS)], axis=1) * (IDX_HEADS ** -0.5)
    limit_row = jnp.full((1, LANES), n_keys, I32)

    for i in range(n_trips):
        off = i * TRIP
        key_ref[off:off + TRIP, :] = _score_trip(kib_scr[off:off + TRIP, :], qis, w_row, limit_row, off, tq)
    _select(key_ref, pos_ref, bias_ref, n_trips, k_sel)
    _attend(bias_ref, n_trips,
            lambda n, off: kg_scr[n, pl.ds(off, TRIP), :],
            lambda n, off: vt_scr[n, :, pl.ds(off, TRIP)],
            qs_ref, acc_ref, m_ref, 1, worst_shift)

    for n in range(N_KV):
        a = acc_ref[n]
        o = (a[0:HEAD_DIM, :] / a[HEAD_DIM:HEAD_DIM + 1, :]).T
        for pair in range(GROUP // 2):
            col = (n * GROUP + 2 * pair) * HEAD_DIM
            o_ref[:, col:col + 2 * HEAD_DIM] = jnp.concatenate(
                [o[(2 * pair) * tq:(2 * pair + 1) * tq, :], o[(2 * pair + 1) * tq:(2 * pair + 2) * tq, :]], axis=1)


def _dsa_sample(p, attn, row0, streams, tq, cache_k, cache_v, cache_ik, k_sel):
    past = cache_k.shape[1]
    assert past % CACHE_FILL == 0
    l_pad = ((past + tq + TRIP - 1) // TRIP) * TRIP
    b0 = row0 // tq
    tok3 = lambda s: (0, b0 + s, 0)
    tok2 = lambda s: (b0 + s, 0)
    slab = lambda s: (s, 0, 0)
    return pl.pallas_call(
        functools.partial(_dsa_sample_body, tq=tq, past=past, k_sel=k_sel),
        grid=(streams,),
        in_specs=[
            pl.BlockSpec((IDX_HEADS, tq, IDX_DIM), tok3),
            pl.BlockSpec((N_HEADS, tq, HEAD_DIM), tok3),
            pl.BlockSpec((tq, LANES), tok2),
            pl.BlockSpec((tq, IDX_DIM), tok2),
            pl.BlockSpec((tq, N_KV * HEAD_DIM), tok2),
            pl.BlockSpec((tq, N_KV * HEAD_DIM), tok2),
            pl.BlockSpec((None, past, N_KV * HEAD_DIM), slab),
            pl.BlockSpec((None, past, N_KV * HEAD_DIM), slab),
            pl.BlockSpec((None, past, IDX_DIM), slab),
            pl.BlockSpec(memory_space=pl.ANY),
        ],
        out_specs=pl.BlockSpec((tq, N_HEADS * HEAD_DIM), tok2),
        out_shape=jax.ShapeDtypeStruct(attn.shape, F32),
        input_output_aliases={9: 0},
        scratch_shapes=[
            pltpu.VMEM((l_pad, LANES), I32), pltpu.VMEM((l_pad, LANES), I32), pltpu.VMEM((l_pad, LANES), BF16),
            pltpu.VMEM((l_pad, IDX_DIM), BF16),
            pltpu.VMEM((N_KV, l_pad, LANES), BF16),
            pltpu.VMEM((N_KV, 2 * HEAD_DIM, l_pad), BF16),
            pltpu.VMEM((N_KV, GROUP * tq, LANES), BF16),
            pltpu.VMEM((N_KV, 2 * HEAD_DIM, GROUP * tq), F32),
            pltpu.VMEM((N_KV, 1, GROUP * tq), F32),
            pltpu.VMEM((N_KV, 1, LANES), F32),
        ],
        compiler_params=_params(("arbitrary",)),
        name="dsa_sample",
    )(p["qi"], p["q"], p["gt"], p["kif"], p["kf"], p["vf"], cache_k, cache_v, cache_ik, attn)


def _log_sigmoid(x):
    return jnp.minimum(x, 0.0) - jnp.log1p(jnp.exp(-jnp.abs(x)))


def _mlstm_body(mq_ref, mk_ref, mv_ref, g_ref, bias_ref, c0_ref, n0_ref, m0_ref, *rest, t):
    h_ref, c_ref, n_ref, m_ref = rest[-4:]

    @pl.when(pl.program_id(1) == 0)
    def _():
        c_ref[...] = c0_ref[...]
        n_ref[...] = n0_ref[...]
        m_ref[...] = m0_ref[...]

    gb = g_ref[...] + bias_ref[0:1, :]
    eye = jnp.where(lax.broadcasted_iota(I32, (LANES, LANES), 0) == lax.broadcasted_iota(I32, (LANES, LANES), 1),
                    1.0, 0.0)
    gbt = lax.dot_general(eye, gb, (((1,), (1,)), ((), ())),
                          precision=lax.Precision.HIGHEST, preferred_element_type=F32)
    r = lax.broadcasted_iota(I32, (t, t), 0)
    c = lax.broadcasted_iota(I32, (t, t), 1)
    causal = c <= r
    for h in range(M_HEADS):
        ig_row = gbt[G_MI + h:G_MI + h + 1, :]
        lf_row = _log_sigmoid(gbt[G_MF + h:G_MF + h + 1, :])
        ig_col = gb[:, G_MI + h:G_MI + h + 1]
        lf_col = _log_sigmoid(gb[:, G_MF + h:G_MF + h + 1])
        b_col = jnp.sum(jnp.where(causal, lf_row, 0.0), axis=1, keepdims=True)
        b_row = jnp.sum(jnp.where(r <= c, lf_col, 0.0), axis=0, keepdims=True)
        b_last = b_col[t - 1:t, :]
        m_prev = m_ref[0, h:h + 1, 0:1]
        d = jnp.where(causal, b_col - b_row + ig_row, -jnp.inf)
        m_inter = b_col + m_prev
        m_t = jnp.maximum(m_inter, jnp.max(d, axis=1, keepdims=True))
        sl = slice(h * M_DIM, (h + 1) * M_DIM)
        q = mq_ref[:, sl]
        k = mk_ref[:, sl] * (M_DIM ** -0.5)
        v = mv_ref[:, sl]
        qb, kb = q.astype(BF16), k.astype(BF16)
        sc = _dot_nt(qb, kb) * jnp.exp(d - m_t)
        decay = jnp.exp(m_inter - m_t)
        c_old = c_ref[0, h]
        n_old = n_ref[0, h:h + 1, :]
        num = _dot(sc.astype(BF16), v.astype(BF16)) + decay * _dot_nt(qb, c_old.astype(BF16))
        den = jnp.sum(sc, axis=1, keepdims=True) + decay * jnp.sum(q * n_old, axis=1, keepdims=True)
        h_ref[:, sl] = num / jnp.maximum(jnp.abs(den), jnp.exp(-m_t))
        m_new = m_t[t - 1:t, :]
        g_col = jnp.exp(b_last - b_col + ig_col - m_new)
        dec_end = jnp.exp(b_last + m_prev - m_new)
        vg_t = (v * g_col).T.astype(BF16)
        c_ref[0, h] = dec_end * c_old + _dot(vg_t, kb)
        n_ref[0, h:h + 1, :] = dec_end * n_old + jnp.sum(g_col * k, axis=0, keepdims=True)
        m_ref[0, h:h + 1, :] = jnp.broadcast_to(m_new, (1, LANES))


def _mlstm(p, row0, batch, seq, t, bias, c0, n0, m0, h_buf=None):
    nc = seq // t
    b0 = row0 // t
    hm = M_HEADS * M_DIM
    n_total = p["mq"].shape[0]
    tok = lambda b, c: (b0 + b * nc + c, 0)
    st4 = lambda b, c: (b, 0, 0, 0)
    st3 = lambda b, c: (b, 0, 0)
    in_specs = [
        pl.BlockSpec((t, hm), tok), pl.BlockSpec((t, hm), tok), pl.BlockSpec((t, hm), tok),
        pl.BlockSpec((t, LANES), tok),
        pl.BlockSpec((8, LANES), lambda b, c: (0, 0)),
        pl.BlockSpec((1, M_HEADS, M_DIM, M_DIM), st4),
        pl.BlockSpec((1, M_HEADS, M_DIM), st3),
        pl.BlockSpec((1, M_HEADS, LANES), st3),
    ]
    args = [p["mq"], p["mk"], p["mv"], p["gt"], bias, c0, n0, m0]
    aliases = {}
    if h_buf is not None:
        in_specs.append(pl.BlockSpec(memory_space=pl.ANY))
        args.append(h_buf)
        aliases = {len(args) - 1: 0}
    return pl.pallas_call(
        functools.partial(_mlstm_body, t=t),
        grid=(batch, nc),
        in_specs=in_specs,
        input_output_aliases=aliases,
        out_specs=[
            pl.BlockSpec((t, hm), tok),
            pl.BlockSpec((1, M_HEADS, M_DIM, M_DIM), st4),
            pl.BlockSpec((1, M_HEADS, M_DIM), st3),
            pl.BlockSpec((1, M_HEADS, LANES), st3),
        ],
        out_shape=[
            jax.ShapeDtypeStruct((n_total, hm), F32),
            jax.ShapeDtypeStruct((batch, M_HEADS, M_DIM, M_DIM), F32),
            jax.ShapeDtypeStruct((batch, M_HEADS, M_DIM), F32),
            jax.ShapeDtypeStruct((batch, M_HEADS, LANES), F32),
        ],
        compiler_params=_params(("arbitrary", "arbitrary")),
        name="mlstm",
    )(*args)


def _merge_body(x_ref, attn_ref, mh_ref, mo_ref, ga_ref, gm_ref, gml_ref, wap_ref, wmp_ref, wo_ref, o_ref):
    a = _dot(attn_ref[...].astype(BF16), wap_ref[...])
    parts = []
    for h in range(M_HEADS):
        sl = slice(h * M_DIM, (h + 1) * M_DIM)
        parts.append(_rms(mh_ref[:, sl], gml_ref[:, sl]))
    hn = jnp.concatenate(parts, axis=1)
    b = _dot((jax.nn.sigmoid(mo_ref[...]) * hn).astype(BF16), wmp_ref[...])
    mix = jax.nn.sigmoid(ga_ref[...]) * a + jax.nn.sigmoid(gm_ref[...]) * b
    o_ref[...] = x_ref[...] + _dot(mix.astype(BF16), wo_ref[...])


def _merge(x, attn, mh, p, g_ml, w_ap, w_mp, w_o, *, tm=256):
    n, d = x.shape
    tm = _tile(n, tm)
    ha, hm = N_HEADS * HEAD_DIM, M_HEADS * M_DIM
    row = lambda i: (i, 0)
    fix = lambda i: (0, 0)
    return pl.pallas_call(
        _merge_body,
        grid=(n // tm,),
        in_specs=[
            pl.BlockSpec((tm, d), row), pl.BlockSpec((tm, ha), row), pl.BlockSpec((tm, hm), row),
            pl.BlockSpec((tm, hm), row), pl.BlockSpec((tm, d), row), pl.BlockSpec((tm, d), row),
            pl.BlockSpec((1, hm), fix), pl.BlockSpec((ha, d), fix), pl.BlockSpec((hm, d), fix),
            pl.BlockSpec((d, d), fix),
        ],
        out_specs=pl.BlockSpec((tm, d), row),
        out_shape=jax.ShapeDtypeStruct((n, d), F32),
        compiler_params=_params(("parallel",)),
        name="merge",
    )(x, attn, mh, p["mo"], p["ga"], p["gm"], g_ml.reshape(1, hm), w_ap, w_mp, w_o)


def _rearrange_w_in(w):
    cuts = [int(c) for c in np.cumsum(SPLIT_SIZES)[:-1]]
    q, k, v, qi, ki, wi, mq, mk, mv, mi, mf, mo, ga, gm = jnp.split(w, cuts, axis=1)
    pad = jnp.zeros((w.shape[0], LANES - IDX_DIM - 3 * IDX_HEADS), w.dtype)
    return jnp.concatenate([q, k, v, qi, ki, wi, mi, mf, pad, mq, mk, mv, mo, ga, gm], axis=1).astype(BF16)


def _rope_tables(pos):
    freqs = ROPE_THETA ** (-jnp.arange(ROT_HALF, dtype=F32) / ROT_HALF)
    ang = pos.astype(F32)[:, None] * freqs[None, :]
    cos, sin = jnp.cos(ang), jnp.sin(ang)
    n = pos.shape[0]
    one = jnp.ones((n, HEAD_DIM - 2 * ROT_HALF), F32)
    zero8 = jnp.zeros((n, ROT_HALF), F32)
    zero = jnp.zeros((n, HEAD_DIM - 2 * ROT_HALF), F32)
    two = lambda a: jnp.concatenate([a, a], axis=1)
    return (two(jnp.concatenate([cos, cos, one], axis=1)),
            two(jnp.concatenate([zero8, sin, zero], axis=1)),
            two(jnp.concatenate([-sin, zero8, zero], axis=1)))


def kernel(x_prompt, x_sample, cache_k, cache_v, cache_idx_k, state_C, state_n, state_m,
           g_ffn1, w_ffn1_gate, w_ffn1_up, w_ffn1_down, g_mix, w_in, b_igate, b_fgate,
           g_mlstm, w_attn_proj, w_mlstm_proj, w_out, g_ffn2, w_ffn2_gate, w_ffn2_up,
           w_ffn2_down, g_final):
    batch, seq, d = x_prompt.shape
    streams, t_s, _ = x_sample.shape
    depth = w_in.shape[0]
    past = cache_k.shape[2]
    n_p, n_s = batch * seq, streams * t_s
    proj_tm = 256
    t_prompt = 256

    x = jnp.concatenate([x_prompt.reshape(n_p, d), x_sample.reshape(n_s, d)], axis=0)

    pos = jnp.concatenate([jnp.arange(seq, dtype=I32),
                           past + (jnp.arange(proj_tm, dtype=I32) % t_s)])
    cos_t, s1_t, s2_t = _rope_tables(pos)
    tiles_per_seq = seq // proj_tm
    n_prompt_tiles = n_p // proj_tm
    tab_block = lambda i: jnp.where(i < n_prompt_tiles, i % tiles_per_seq, tiles_per_seq)

    k_sel_p = min(TOPK_MAX, seq // 4)
    k_sel_s = min(TOPK_MAX, (past + t_s) // 4)

    kp, vp, ikp, cp, np_, mp = [], [], [], [], [], []
    ks, vs, iks, cs, ns, ms = [], [], [], [], [], []
    for l in range(depth):
        last = l == depth - 1
        x = _ffn(x, g_ffn1[l], w_ffn1_gate[l].astype(BF16), w_ffn1_up[l].astype(BF16),
                 w_ffn1_down[l].astype(BF16), g_final, final_norm=False)
        p = _proj(x, g_mix[l], _rearrange_w_in(w_in[l]), cos_t, s1_t, s2_t, tab_block, tm=proj_tm)

        attn = _dsa_prompt(p, batch, seq, k_sel_p)
        attn = _dsa_sample(p, attn, n_p, streams, t_s,
                           cache_k[l].reshape(streams, past, N_KV * HEAD_DIM),
                           cache_v[l].reshape(streams, past, N_KV * HEAD_DIM),
                           cache_idx_k[l], k_sel_s)

        bias = jnp.zeros((8, LANES), F32)
        bias = bias.at[0, G_MI:G_MI + M_HEADS].set(b_igate[l]).at[0, G_MF:G_MF + M_HEADS].set(b_fgate[l])
        mh, c_p, n_pr, m_p = _mlstm(
            p, 0, batch, seq, t_prompt, bias,
            jnp.zeros((batch, M_HEADS, M_DIM, M_DIM), F32), jnp.zeros((batch, M_HEADS, M_DIM), F32),
            jnp.full((batch, M_HEADS, LANES), M_NEG, F32))
        mh, c_s, n_sm, m_s = _mlstm(
            p, n_p, streams, t_s, t_s, bias, state_C[l], state_n[l],
            jnp.broadcast_to(state_m[l][:, :, None], (streams, M_HEADS, LANES)), h_buf=mh)

        x = _merge(x, attn, mh, p,
                   g_mlstm[l], w_attn_proj[l].astype(BF16), w_mlstm_proj[l].astype(BF16), w_out[l].astype(BF16))
        x = _ffn(x, g_ffn2[l], w_ffn2_gate[l].astype(BF16), w_ffn2_up[l].astype(BF16),
                 w_ffn2_down[l].astype(BF16), g_final, final_norm=last, split_rows=n_p if last else None)

        kp.append(p["kf"][:n_p].reshape(batch, seq, N_KV, HEAD_DIM))
        vp.append(p["vf"][:n_p].reshape(batch, seq, N_KV, HEAD_DIM))
        ikp.append(p["kif"][:n_p].reshape(batch, seq, IDX_DIM))
        cp.append(c_p); np_.append(n_pr); mp.append(m_p[:, :, 0])
        ks.append(p["kf"][n_p:].reshape(streams, t_s, N_KV, HEAD_DIM))
        vs.append(p["vf"][n_p:].reshape(streams, t_s, N_KV, HEAD_DIM))
        iks.append(p["kif"][n_p:].reshape(streams, t_s, IDX_DIM))
        cs.append(c_s); ns.append(n_sm); ms.append(m_s[:, :, 0])

    st = jnp.stack
    y_p, y_s = x
    return (y_p.reshape(batch, seq, d), y_s.reshape(streams, t_s, d),
            st(kp), st(vp), st(ikp), st(cp), st(np_), st(mp),
            st(ks), st(vs), st(iks), st(cs), st(ns), st(ms))
```
